```python
import jax, jax.numpy as jnp
from jax import lax
import numpy as np

D_MODEL = 1024
BATCH = 16
SEQ = 4096
DEPTH = 1

CHUNK = 64
N_MEM = 256
GDN_HEAD_DIM = 128
GDN_WIDTH = D_MODEL
GDN_HEADS = GDN_WIDTH // GDN_HEAD_DIM
CONV_WIDTH = 4
GLA_HEADS = 4
GLA_KEY_WIDTH = D_MODEL // 2
GLA_VAL_WIDTH = D_MODEL
GLA_KEY_DIM = GLA_KEY_WIDTH // GLA_HEADS
GLA_VAL_DIM = GLA_VAL_WIDTH // GLA_HEADS
GLA_GATE_RANK = 16
GLA_GATE_TAU = 16.0
XATTN_HEADS = 4
XATTN_HEAD_DIM = D_MODEL // XATTN_HEADS
D_FF = 4 * D_MODEL
NORM_EPS = 1e-6
IN_SIZES = (GDN_WIDTH, GDN_WIDTH, GDN_WIDTH, GDN_WIDTH, GDN_HEADS, GDN_HEADS,
            GLA_KEY_WIDTH, GLA_KEY_WIDTH, GLA_VAL_WIDTH, GLA_VAL_WIDTH, GLA_GATE_RANK,
            D_MODEL, D_MODEL)
IN_WIDTH = 4 * GDN_WIDTH + 2 * GDN_HEADS + 2 * GLA_KEY_WIDTH + 2 * GLA_VAL_WIDTH + GLA_GATE_RANK + 2 * D_MODEL

kernel_name = "hybrid_gdn_gla_gated_merge_xattn_sqrelu"


def rmsnorm(x, g):
    xf = x.astype(jnp.float32)
    y = xf * lax.rsqrt(jnp.mean(xf * xf, axis=-1, keepdims=True) + NORM_EPS)
    return (y * g.astype(jnp.float32)).astype(x.dtype)


def l2norm(x):
    return x * lax.rsqrt(jnp.sum(x * x, axis=-1, keepdims=True) + NORM_EPS)


def causal_depthwise_conv(x, w):
    c = x.shape[-1]
    return lax.conv_general_dilated(
        x, w.astype(x.dtype)[:, None, :], window_strides=(1,),
        padding=[(w.shape[0] - 1, 0)], dimension_numbers=("NWC", "WIO", "NWC"),
        feature_group_count=c)


def to_chunks(t, n_heads):
    b, s, _ = t.shape
    return t.reshape(b, s // CHUNK, CHUNK, n_heads, -1).transpose(0, 3, 1, 2, 4)


def scalar_chunks(t):
    b, s, h = t.shape
    return t.reshape(b, s // CHUNK, CHUNK, h).transpose(0, 3, 1, 2)


def from_chunks(t):
    b, h, n, c, d = t.shape
    return t.transpose(0, 2, 3, 1, 4).reshape(b, n * c, h, d)


def gated_delta_rule_chunked(q, k, v, g, beta):
    dk, dv = k.shape[-1], v.shape[-1]
    idx = jnp.arange(CHUNK)
    strict = idx[:, None] > idx[None, :]
    incl = idx[:, None] >= idx[None, :]
    gc = jnp.cumsum(g, axis=-1)
    decay = jnp.exp(jnp.where(incl, gc[..., :, None] - gc[..., None, :], -jnp.inf))
    kk = jnp.einsum('bhnid,bhnjd->bhnij', k, k)
    lower = jnp.where(strict, beta[..., :, None] * kk * decay, 0.0)
    rhs = jnp.concatenate([beta[..., None] * v, (beta * jnp.exp(gc))[..., None] * k], axis=-1)
    sol = lax.linalg.triangular_solve(lower, rhs, left_side=True, lower=True, unit_diagonal=True)
    u, w = sol[..., :dv], sol[..., dv:]
    a_qk = jnp.einsum('bhnid,bhnjd->bhnij', q, k) * decay
    q_dec = q * jnp.exp(gc)[..., None]
    k_dec = k * jnp.exp(gc[..., -1:] - gc)[..., None]
    g_end = jnp.exp(gc[..., -1])
    xs = tuple(jnp.moveaxis(t, 2, 0) for t in (u, w, q_dec, a_qk, k_dec, g_end))

    def step(state, inp):
        u_c, w_c, q_c, a_c, k_c, ge = inp
        delta = u_c - jnp.einsum('bhcd,bhde->bhce', w_c, state)
        o = jnp.einsum('bhcd,bhde->bhce', q_c, state) + jnp.einsum('bhij,bhje->bhie', a_c, delta)
        state = ge[..., None, None] * state + jnp.einsum('bhcd,bhce->bhde', k_c, delta)
        return state, o

    s0 = jnp.zeros(q.shape[:2] + (dk, dv), q.dtype)
    _, o = lax.scan(step, s0, xs)
    return jnp.moveaxis(o, 0, 2)


def gla_chunked(q, k, v, gk):
    dk, dv = k.shape[-1], v.shape[-1]
    idx = jnp.arange(CHUNK)
    incl = idx[:, None] >= idx[None, :]
    bc = jnp.cumsum(gk, axis=-2)
    b_ref = bc[..., CHUNK // 2:CHUNK // 2 + 1, :]
    a_qk = jnp.einsum('bhnid,bhnjd->bhnij', q * jnp.exp(bc - b_ref), k * jnp.exp(b_ref - bc))
    o_intra = jnp.einsum('bhnij,bhnje->bhnie', jnp.where(incl, a_qk, 0.0), v)
    q_dec = q * jnp.exp(bc)
    k_dec = k * jnp.exp(bc[..., -1:, :] - bc)
    g_end = jnp.exp(bc[..., -1, :])
    xs = tuple(jnp.moveaxis(t, 2, 0) for t in (q_dec, o_intra, k_dec, v, g_end))

    def step(state, inp):
        q_c, oi_c, k_c, v_c, ge = inp
        o = jnp.einsum('bhcd,bhde->bhce', q_c, state) + oi_c
        state = ge[..., :, None] * state + jnp.einsum('bhcd,bhce->bhde', k_c, v_c)
        return state, o

    s0 = jnp.zeros(q.shape[:2] + (dk, dv), q.dtype)
    _, o = lax.scan(step, s0, xs)
    return jnp.moveaxis(o, 0, 2)


def setup_inputs(seed: int = 0) -> dict:
    key = jax.random.key(seed)
    ks = iter(jax.random.split(key, 32))
    nrm = lambda shape, scale: jax.random.normal(next(ks), shape, jnp.float32) * scale
    gain = lambda n: 1.0 + nrm((DEPTH, n), 0.02)
    a_log = jnp.log(jax.random.uniform(next(ks), (DEPTH, GDN_HEADS), jnp.float32, 1.0, 16.0))
    dt = jnp.exp(jax.random.uniform(next(ks), (DEPTH, GDN_HEADS), jnp.float32, np.log(1e-3), np.log(1e-1)))
    dt_bias = dt + jnp.log(-jnp.expm1(-dt))
    return {
        "x": nrm((BATCH, SEQ, D_MODEL), 1.0),
        "mem": nrm((BATCH, N_MEM, D_MODEL), 1.0),
        "norm_mix_g": gain(D_MODEL),
        "w_in": nrm((DEPTH, D_MODEL, IN_WIDTH), D_MODEL ** -0.5),
        "gdn_conv_w": nrm((DEPTH, CONV_WIDTH, 3 * GDN_WIDTH), CONV_WIDTH ** -0.5),
        "gdn_a_log": a_log,
        "gdn_dt_bias": dt_bias,
        "gdn_norm_g": gain(GDN_HEAD_DIM),
        "gla_w_gate2": nrm((DEPTH, GLA_GATE_RANK, GLA_KEY_WIDTH), GLA_GATE_RANK ** -0.5),
        "gla_b_gate": nrm((DEPTH, GLA_KEY_WIDTH), 0.1),
        "gla_norm_g": gain(GLA_VAL_DIM),
        "w_branch_gdn": nrm((DEPTH, GDN_WIDTH, D_MODEL), GDN_WIDTH ** -0.5),
        "w_branch_gla": nrm((DEPTH, GLA_VAL_WIDTH, D_MODEL), GLA_VAL_WIDTH ** -0.5),
        "w_out": nrm((DEPTH, D_MODEL, D_MODEL), D_MODEL ** -0.5),
        "norm_xattn_g": gain(D_MODEL),
        "norm_mem_g": gain(D_MODEL),
        "xattn_wq": nrm((DEPTH, D_MODEL, D_MODEL), D_MODEL ** -0.5),
        "xattn_wk": nrm((DEPTH, D_MODEL, D_MODEL), D_MODEL ** -0.5),
        "xattn_wv": nrm((DEPTH, D_MODEL, D_MODEL), D_MODEL ** -0.5),
        "xattn_wo": nrm((DEPTH, D_MODEL, D_MODEL), D_MODEL ** -0.5),
        "norm_mlp_g": gain(D_MODEL),
        "mlp_w1": nrm((DEPTH, D_MODEL, D_FF), D_MODEL ** -0.5),
        "mlp_w2": nrm((DEPTH, D_FF, D_MODEL), D_FF ** -0.5),
        "norm_final_g": 1.0 + nrm((D_MODEL,), 0.02),
    }


def reference(x, mem, norm_mix_g, w_in, gdn_conv_w, gdn_a_log, gdn_dt_bias, gdn_norm_g,
              gla_w_gate2, gla_b_gate, gla_norm_g, w_branch_gdn, w_branch_gla, w_out,
              norm_xattn_g, norm_mem_g, xattn_wq, xattn_wk, xattn_wv, xattn_wo,
              norm_mlp_g, mlp_w1, mlp_w2, norm_final_g):
    dt = x.dtype
    f32 = jnp.float32
    bsz, seq, _ = x.shape
    split_at = [int(v) for v in np.cumsum(IN_SIZES)[:-1]]
    for i in range(DEPTH):
        h = rmsnorm(x, norm_mix_g[i])
        proj = h @ w_in[i].astype(dt)
        (gq, gk_, gv, gz, ga, gb, lq, lk, lv, lr, lgate, gate_a, gate_b) = jnp.split(proj, split_at, axis=-1)

        qkv = jax.nn.silu(causal_depthwise_conv(jnp.concatenate([gq, gk_, gv], -1), gdn_conv_w[i]))
        cq, ck, cv = jnp.split(qkv.astype(f32), [GDN_WIDTH, 2 * GDN_WIDTH], axis=-1)
        q_a = l2norm(to_chunks(cq, GDN_HEADS)) * (GDN_HEAD_DIM ** -0.5)
        k_a = l2norm(to_chunks(ck, GDN_HEADS))
        v_a = to_chunks(cv, GDN_HEADS)
        g_a = -jnp.exp(gdn_a_log[i].astype(f32)) * jax.nn.softplus(ga.astype(f32) + gdn_dt_bias[i].astype(f32))
        beta_a = jax.nn.sigmoid(gb.astype(f32))
        o_a = gated_delta_rule_chunked(q_a, k_a, v_a, scalar_chunks(g_a), scalar_chunks(beta_a))
        o_a = rmsnorm(from_chunks(o_a).astype(dt), gdn_norm_g[i]) * jax.nn.silu(gz.reshape(bsz, seq, GDN_HEADS, GDN_HEAD_DIM))
        y_a = o_a.reshape(bsz, seq, GDN_WIDTH) @ w_branch_gdn[i].astype(dt)

        log_fg = jax.nn.log_sigmoid((lgate @ gla_w_gate2[i].astype(dt)).astype(f32) + gla_b_gate[i].astype(f32)) / GLA_GATE_TAU
        q_b = to_chunks(lq.astype(f32), GLA_HEADS) * (GLA_KEY_DIM ** -0.5)
        k_b = to_chunks(lk.astype(f32), GLA_HEADS)
        v_b = to_chunks(lv.astype(f32), GLA_HEADS)
        o_b = gla_chunked(q_b, k_b, v_b, to_chunks(log_fg, GLA_HEADS))
        o_b = rmsnorm(from_chunks(o_b).astype(dt), gla_norm_g[i]) * jax.nn.silu(lr.reshape(bsz, seq, GLA_HEADS, GLA_VAL_DIM))
        y_b = o_b.reshape(bsz, seq, GLA_VAL_WIDTH) @ w_branch_gla[i].astype(dt)

        merged = jax.nn.sigmoid(gate_a) * y_a + jax.nn.sigmoid(gate_b) * y_b
        x = x + merged @ w_out[i].astype(dt)

        h = rmsnorm(x, norm_xattn_g[i])
        m = rmsnorm(mem.astype(dt), norm_mem_g[i])
        q = (h @ xattn_wq[i].astype(dt)).reshape(bsz, seq, XATTN_HEADS, XATTN_HEAD_DIM)
        k = (m @ xattn_wk[i].astype(dt)).reshape(bsz, -1, XATTN_HEADS, XATTN_HEAD_DIM)
        v = (m @ xattn_wv[i].astype(dt)).reshape(bsz, -1, XATTN_HEADS, XATTN_HEAD_DIM)
        s = jnp.einsum('bshd,bmhd->bhsm', q, k).astype(f32) * (XATTN_HEAD_DIM ** -0.5)
        p = jax.nn.softmax(s, axis=-1).astype(dt)
        o = jnp.einsum('bhsm,bmhd->bshd', p, v).reshape(bsz, seq, D_MODEL)
        x = x + o @ xattn_wo[i].astype(dt)

        h = rmsnorm(x, norm_mlp_g[i])
        x = x + jnp.square(jax.nn.relu(h @ mlp_w1[i].astype(dt))) @ mlp_w2[i].astype(dt)
    return rmsnorm(x, norm_final_g)
```

```python
import functools

import jax
import jax.numpy as jnp
from jax import lax
from jax.experimental import pallas as pl
from jax.experimental.pallas import tpu as pltpu

F32 = jnp.float32
BF16 = jnp.bfloat16

CHUNK = 64
CONV_TAPS = 4
GLA_GATE_TAU = 16.0
XATTN_HEADS = 4
NORM_EPS = 1e-6
LANES = 128
CARRY_ROWS = 8
INV_LEAF = 16

VMEM_LIMIT = 48 * 1024 * 1024


def _params(semantics):
    return pltpu.CompilerParams(dimension_semantics=semantics, vmem_limit_bytes=VMEM_LIMIT)


def _dot(a, b):
    return jnp.dot(a.astype(BF16), b.astype(BF16), preferred_element_type=F32)


def _dot_nt(a, b):
    return lax.dot_general(a.astype(BF16), b.astype(BF16), (((1,), (1,)), ((), ())),
                           preferred_element_type=F32)


def _split(a):
    hi = a.astype(BF16)
    lo = (a - hi.astype(F32)).astype(BF16)
    return hi, lo


def _dot_x3(a, b):
    ah, al = _split(a)
    bh, bl = _split(b)
    d = functools.partial(jnp.dot, preferred_element_type=F32)
    return d(ah, bh) + (d(ah, bl) + d(al, bh))


def _dot_exact_lhs(tri_bf16, b):
    bh, bl = _split(b)
    d = functools.partial(jnp.dot, preferred_element_type=F32)
    return d(tri_bf16, bh) + d(tri_bf16, bl)


def _sigmoid(x):
    return 1.0 / (1.0 + jnp.exp(-x))


def _softplus(x):
    return jnp.maximum(x, 0.0) + jnp.log1p(jnp.exp(-jnp.abs(x)))


def _rms(x, g):
    return x * lax.rsqrt(jnp.mean(x * x, axis=-1, keepdims=True) + NORM_EPS) * g


def _chunk_cumsum_matrix(ts):
    r = lax.broadcasted_iota(jnp.int32, (ts, ts), 0)
    c = lax.broadcasted_iota(jnp.int32, (ts, ts), 1)
    return jnp.where((r >= c) & ((r // CHUNK) == (c // CHUNK)), 1.0, 0.0).astype(BF16)


def _inproj_kernel(x_ref, g_ref, wbig_ref, wsm_ref, big_ref, sm_ref, h_scr):
    @pl.when(pl.program_id(1) == 0)
    def _():
        h = _rms(x_ref[...], g_ref[...]).astype(BF16)
        h_scr[...] = h
        sm_ref[...] = jnp.dot(h, wsm_ref[...], preferred_element_type=F32)

    big_ref[...] = jnp.dot(h_scr[...], wbig_ref[...], preferred_element_type=F32).astype(big_ref.dtype)


def _inproj(x2, g, w_big, w_small, *, tm, tn):
    t, d = x2.shape
    n = w_big.shape[1]
    return pl.pallas_call(
        _inproj_kernel,
        grid=(t // tm, n // tn),
        in_specs=[
            pl.BlockSpec((tm, d), lambda i, j: (i, 0)),
            pl.BlockSpec((1, d), lambda i, j: (0, 0)),
            pl.BlockSpec((d, tn), lambda i, j: (0, j)),
            pl.BlockSpec((d, LANES), lambda i, j: (0, 0)),
        ],
        out_specs=[
            pl.BlockSpec((tm, tn), lambda i, j: (i, j)),
            pl.BlockSpec((tm, LANES), lambda i, j: (i, 0)),
        ],
        out_shape=[
            jax.ShapeDtypeStruct((t, n), BF16),
            jax.ShapeDtypeStruct((t, LANES), F32),
        ],
        scratch_shapes=[pltpu.VMEM((tm, d), BF16)],
        compiler_params=_params(("parallel", "arbitrary")),
        name="inproj",
    )(x2, g, w_big, w_small)


def _unit_lower_inverse(low, eye, leaf_mask):
    mm = _dot_x3
    dg = jnp.where(leaf_mask, low, 0.0)
    off = low - dg
    d2 = mm(dg, dg)
    p = eye - dg
    p = p + mm(p, d2)
    d4 = mm(d2, d2)
    p = p + mm(p, d4)
    d8 = mm(d4, d4)
    p = p + mm(p, d8)
    n = mm(p, off)
    n2 = mm(n, n)
    r = eye - n
    r = r + mm(r, n2)
    return mm(r, p)


def _gdn_kernel(q_ref, k_ref, v_ref, z_ref, sm_ref, cw_ref, alog_ref, dtb_ref, ng_ref, o_ref,
                xbuf, qn_scr, kn_scr, vv_scr, gc_scr, bt_scr, s_scr, *, ts, nh, hd):
    w = nh * hd
    nch = ts // CHUNK

    @pl.when(pl.program_id(1) == 0)
    def _():
        xbuf[0:CARRY_ROWS, :] = jnp.zeros((CARRY_ROWS, 3 * w), F32)
        s_scr[...] = jnp.zeros_like(s_scr)

    for part, (src, dst) in enumerate(((q_ref, qn_scr), (k_ref, kn_scr), (v_ref, vv_scr))):
        xbuf[CARRY_ROWS:CARRY_ROWS + ts, part * w:(part + 1) * w] = src[...].astype(F32)
        for h in range(nh):
            cs = slice(part * w + h * hd, part * w + (h + 1) * hd)
            y = None
            for i in range(CONV_TAPS):
                r0 = CARRY_ROWS - (CONV_TAPS - 1) + i
                term = cw_ref[i:i + 1, cs] * xbuf[r0:r0 + ts, cs]
                y = term if y is None else y + term
            y = y * _sigmoid(y)
            if part < 2:
                y = y * lax.rsqrt(jnp.sum(y * y, axis=-1, keepdims=True) + NORM_EPS)
            if part == 0:
                y = y * (hd ** -0.5)
            dst[:, h * hd:(h + 1) * hd] = y
    xbuf[0:CARRY_ROWS, :] = xbuf[ts:ts + CARRY_ROWS, :]

    sm = sm_ref[...]
    g = -jnp.exp(alog_ref[...]) * _softplus(sm + dtb_ref[...])
    gc_scr[...] = _dot_exact_lhs(_chunk_cumsum_matrix(ts), g)
    bt_scr[...] = _sigmoid(sm)

    row = lax.broadcasted_iota(jnp.int32, (CHUNK, CHUNK), 0)
    col = lax.broadcasted_iota(jnp.int32, (CHUNK, CHUNK), 1)
    incl = row >= col
    strict = row > col
    leaf_mask = (row // INV_LEAF) == (col // INV_LEAF)
    eye = jnp.where(row == col, 1.0, 0.0).astype(F32)
    ng = ng_ref[...]

    def chunk_body(c, carry):
        r0 = pl.multiple_of(c * CHUNK, CHUNK)
        rows = pl.ds(r0, CHUNK)
        gcc = gc_scr[rows, :]
        gcr = gcc.T
        btc = bt_scr[rows, :]
        for h in range(nh):
            cs = slice(h * hd, (h + 1) * hd)
            q = qn_scr[rows, cs]
            k = kn_scr[rows, cs]
            v = vv_scr[rows, cs]
            gcol = gcc[:, h:h + 1]
            grow = gcr[h:h + 1, :]
            bcol = btc[:, nh + h:nh + h + 1]
            dec = jnp.exp(jnp.where(incl, gcol - grow, -jnp.inf))
            gend = gcol[CHUNK - 1:CHUNK, :]
            egc = jnp.exp(gcol)
            ekd = jnp.exp(gend - gcol)
            eend = jnp.exp(gend)

            qk_kk = _dot_nt(jnp.concatenate([q, k], axis=0), k)
            qk = qk_kk[:CHUNK]
            kk = qk_kk[CHUNK:]
            low = jnp.where(strict, bcol * kk * dec, 0.0)
            tinv = _unit_lower_inverse(low, eye, leaf_mask)
            rhs = jnp.concatenate([bcol * v, (bcol * egc) * k], axis=1)
            uw = _dot(tinv, rhs)
            u = uw[:, :hd]
            wmat = uw[:, hd:]
            s_old = s_scr[h]
            ws_qs = _dot(jnp.concatenate([wmat, q * egc], axis=0), s_old)
            delta = u - ws_qs[:CHUNK]
            o = ws_qs[CHUNK:] + _dot(qk * dec, delta)
            s_scr[h] = eend * s_old + _dot((k * ekd).T, delta)

            z = z_ref[rows, cs].astype(F32)
            o_ref[rows, cs] = (_rms(o, ng) * (z * _sigmoid(z))).astype(o_ref.dtype)
        return carry

    lax.fori_loop(0, nch, chunk_body, 0)


def _gdn(proj, small, conv_w, alog, dtb, norm_g, *, bsz, seq, ts, nh, hd):
    t = bsz * seq
    nt = seq // ts
    w = nh * hd
    tok = lambda cb: pl.BlockSpec((ts, w), lambda b, j: (b * nt + j, cb))
    full = lambda shape: pl.BlockSpec(shape, lambda b, j: (0, 0))
    return pl.pallas_call(
        functools.partial(_gdn_kernel, ts=ts, nh=nh, hd=hd),
        grid=(bsz, nt),
        in_specs=[tok(0), tok(1), tok(2), tok(3),
                  pl.BlockSpec((ts, LANES), lambda b, j: (b * nt + j, 0)),
                  full((CONV_TAPS, 3 * w)), full((1, LANES)), full((1, LANES)), full((1, hd))],
        out_specs=pl.BlockSpec((ts, w), lambda b, j: (b * nt + j, 0)),
        out_shape=jax.ShapeDtypeStruct((t, w), BF16),
        scratch_shapes=[
            pltpu.VMEM((CARRY_ROWS + ts, 3 * w), F32),
            pltpu.VMEM((ts, w), F32), pltpu.VMEM((ts, w), F32), pltpu.VMEM((ts, w), F32),
            pltpu.VMEM((ts, LANES), F32), pltpu.VMEM((ts, LANES), F32),
            pltpu.VMEM((nh, hd, hd), F32),
        ],
        compiler_params=_params(("parallel", "arbitrary")),
        name="gdn",
    )(proj, proj, proj, proj, small, conv_w, alog, dtb, norm_g)


def _gla_kernel(q_ref, k_ref, v_ref, r_ref, sm_ref, wg_ref, bg_ref, ng_ref, o_ref,
                bc_scr, st_scr, *, ts, nh, dk, dv):
    nch = ts // CHUNK

    @pl.when(pl.program_id(1) == 0)
    def _():
        st_scr[...] = jnp.zeros_like(st_scr)

    lg = _dot(sm_ref[...], wg_ref[...]) + bg_ref[...]
    log_fg = (jnp.minimum(lg, 0.0) - jnp.log1p(jnp.exp(-jnp.abs(lg)))) / GLA_GATE_TAU
    bc_scr[...] = _dot_exact_lhs(_chunk_cumsum_matrix(ts), log_fg)

    row = lax.broadcasted_iota(jnp.int32, (CHUNK, CHUNK), 0)
    col = lax.broadcasted_iota(jnp.int32, (CHUNK, CHUNK), 1)
    incl = row >= col
    ng = ng_ref[...]
    scale = dk ** -0.5

    def chunk_body(c, carry):
        r0 = pl.multiple_of(c * CHUNK, CHUNK)
        rows = pl.ds(r0, CHUNK)
        bc = bc_scr[rows, :]
        bref = bc[CHUNK // 2:CHUNK // 2 + 1, :]
        bend = bc[CHUNK - 1:CHUNK, :]
        q = q_ref[rows, :].astype(F32) * scale
        k = k_ref[rows, :].astype(F32)
        qe = q * jnp.exp(bc - bref)
        ke = k * jnp.exp(bref - bc)
        qd = q * jnp.exp(bc)
        kd = k * jnp.exp(bend - bc)
        ge = jnp.exp(bend)
        for h in range(nh):
            ks = slice(h * dk, (h + 1) * dk)
            vs = slice(h * dv, (h + 1) * dv)
            a = jnp.where(incl, _dot_nt(qe[:, ks], ke[:, ks]), 0.0)
            v = v_ref[rows, vs].astype(F32)
            st = st_scr[h]
            o = _dot_nt(qd[:, ks], st) + _dot(a, v)
            st_scr[h] = ge[:, ks] * st + _dot(v.T, kd[:, ks])
            r = r_ref[rows, vs].astype(F32)
            o_ref[rows, vs] = (_rms(o, ng) * (r * _sigmoid(r))).astype(o_ref.dtype)
        return carry

    lax.fori_loop(0, nch, chunk_body, 0)


def _gla(proj, small, w_gate, b_gate, norm_g, *, bsz, seq, ts, nh, dk, dv, col0):
    t = bsz * seq
    nt = seq // ts
    kw, vw = nh * dk, nh * dv
    spec = lambda width, start: pl.BlockSpec((ts, width), lambda b, j: (b * nt + j, start // width))
    full = lambda shape: pl.BlockSpec(shape, lambda b, j: (0, 0))
    return pl.pallas_call(
        functools.partial(_gla_kernel, ts=ts, nh=nh, dk=dk, dv=dv),
        grid=(bsz, nt),
        in_specs=[spec(kw, col0), spec(kw, col0 + kw), spec(vw, col0 + 2 * kw), spec(vw, col0 + 2 * kw + vw),
                  pl.BlockSpec((ts, LANES), lambda b, j: (b * nt + j, 0)),
                  full((LANES, kw)), full((1, kw)), full((1, dv))],
        out_specs=pl.BlockSpec((ts, vw), lambda b, j: (b * nt + j, 0)),
        out_shape=jax.ShapeDtypeStruct((t, vw), BF16),
        scratch_shapes=[pltpu.VMEM((ts, kw), F32), pltpu.VMEM((nh, dv, dk), F32)],
        compiler_params=_params(("parallel", "arbitrary")),
        name="gla",
    )(proj, proj, proj, proj, small, w_gate, b_gate, norm_g)


def _merge_kernel(x_ref, oa_ref, ob_ref, ga_ref, gb_ref, wa_ref, wb_ref, wo_ref, out_ref):
    ya = jnp.dot(oa_ref[...], wa_ref[...], preferred_element_type=F32)
    yb = jnp.dot(ob_ref[...], wb_ref[...], preferred_element_type=F32)
    merged = _sigmoid(ga_ref[...].astype(F32)) * ya + _sigmoid(gb_ref[...].astype(F32)) * yb
    out_ref[...] = x_ref[...] + _dot(merged, wo_ref[...])


def _merge(x2, o_a, o_b, proj, w_a, w_b, w_o, *, tm, gate_col):
    t, d = x2.shape
    tok = lambda cb: pl.BlockSpec((tm, d), lambda i: (i, cb))
    wsp = pl.BlockSpec((d, d), lambda i: (0, 0))
    return pl.pallas_call(
        _merge_kernel,
        grid=(t // tm,),
        in_specs=[tok(0), tok(0), tok(0), tok(gate_col // d), tok(gate_col // d + 1), wsp, wsp, wsp],
        out_specs=tok(0),
        out_shape=jax.ShapeDtypeStruct((t, d), F32),
        compiler_params=_params(("parallel",)),
        name="merge",
    )(x2, o_a, o_b, proj, proj, w_a, w_b, w_o)


def _memkv_kernel(m_ref, g_ref, wk_ref, wv_ref, k_ref, v_ref):
    m = _rms(m_ref[...], g_ref[...]).astype(BF16)
    k_ref[...] = jnp.dot(m, wk_ref[...], preferred_element_type=F32).astype(BF16)
    v_ref[...] = jnp.dot(m, wv_ref[...], preferred_element_type=F32).astype(BF16)


def _memkv(mem2, g, wk, wv, *, n_mem):
    t, d = mem2.shape
    tok = pl.BlockSpec((n_mem, d), lambda b: (b, 0))
    wsp = pl.BlockSpec((d, d), lambda b: (0, 0))
    return pl.pallas_call(
        _memkv_kernel,
        grid=(t // n_mem,),
        in_specs=[tok, pl.BlockSpec((1, d), lambda b: (0, 0)), wsp, wsp],
        out_specs=[tok, tok],
        out_shape=[jax.ShapeDtypeStruct((t, d), BF16)] * 2,
        compiler_params=_params(("parallel",)),
        name="memkv",
    )(mem2, g, wk, wv)


def _xattn_kernel(x_ref, g_ref, k_ref, v_ref, wq_ref, wo_ref, out_ref, *, nh):
    x = x_ref[...]
    d = x.shape[-1]
    hd = d // nh
    q = _dot(_rms(x, g_ref[...]), wq_ref[...])
    heads = []
    for h in range(nh):
        cs = slice(h * hd, (h + 1) * hd)
        s = _dot_nt(q[:, cs], k_ref[:, cs]) * (hd ** -0.5)
        e = jnp.exp(s - jnp.max(s, axis=-1, keepdims=True))
        p = e / jnp.sum(e, axis=-1, keepdims=True)
        heads.append(_dot(p, v_ref[:, cs]))
    o = jnp.concatenate(heads, axis=1)
    out_ref[...] = x + _dot(o, wo_ref[...])


def _xattn(x2, g, k_mem, v_mem, wq, wo, *, bsz, seq, tm, n_mem):
    t, d = x2.shape
    nt = seq // tm
    tok = pl.BlockSpec((tm, d), lambda b, j: (b * nt + j, 0))
    mem = pl.BlockSpec((n_mem, d), lambda b, j: (b, 0))
    wsp = pl.BlockSpec((d, d), lambda b, j: (0, 0))
    return pl.pallas_call(
        functools.partial(_xattn_kernel, nh=XATTN_HEADS),
        grid=(bsz, nt),
        in_specs=[tok, pl.BlockSpec((1, d), lambda b, j: (0, 0)), mem, mem, wsp, wsp],
        out_specs=tok,
        out_shape=jax.ShapeDtypeStruct((t, d), F32),
        compiler_params=_params(("parallel", "parallel")),
        name="xattn",
    )(x2, g, k_mem, v_mem, wq, wo)


def _mlp_kernel(x_ref, g_ref, w1_ref, w2_ref, gf_ref, out_ref, h_scr, acc_scr, *, final_norm):
    j = pl.program_id(1)

    @pl.when(j == 0)
    def _():
        h_scr[...] = _rms(x_ref[...], g_ref[...]).astype(BF16)
        acc_scr[...] = x_ref[...]

    a = jnp.dot(h_scr[...], w1_ref[...], preferred_element_type=F32)
    acc_scr[...] += _dot(jnp.square(jnp.maximum(a, 0.0)), w2_ref[...])

    @pl.when(j == pl.num_programs(1) - 1)
    def _():
        y = acc_scr[...]
        out_ref[...] = _rms(y, gf_ref[...]) if final_norm else y


def _mlp(x2, g, w1, w2, g_final, *, tm, tf, final_norm):
    t, d = x2.shape
    ff = w1.shape[1]
    tok = pl.BlockSpec((tm, d), lambda i, j: (i, 0))
    vec = pl.BlockSpec((1, d), lambda i, j: (0, 0))
    return pl.pallas_call(
        functools.partial(_mlp_kernel, final_norm=final_norm),
        grid=(t // tm, ff // tf),
        in_specs=[tok, vec, pl.BlockSpec((d, tf), lambda i, j: (0, j)),
                  pl.BlockSpec((tf, d), lambda i, j: (j, 0)), vec],
        out_specs=tok,
        out_shape=jax.ShapeDtypeStruct((t, d), F32),
        scratch_shapes=[pltpu.VMEM((tm, d), BF16), pltpu.VMEM((tm, d), F32)],
        compiler_params=_params(("parallel", "arbitrary")),
        name="mlp",
    )(x2, g, w1, w2, g_final)


def _pad_lanes(v, offset):
    return jnp.zeros((1, LANES), F32).at[0, offset:offset + v.shape[0]].set(v.astype(F32))


def _tile(n, pref):
    while n % pref:
        pref //= 2
    return pref


def kernel(x, mem, norm_mix_g, w_in, gdn_conv_w, gdn_a_log, gdn_dt_bias, gdn_norm_g, gla_w_gate2, gla_b_gate, gla_norm_g, w_branch_gdn, w_branch_gla, w_out, norm_xattn_g, norm_mem_g, xattn_wq, xattn_wk, xattn_wv, xattn_wo, norm_mlp_g, mlp_w1, mlp_w2, norm_final_g):
    bsz, seq, d = x.shape
    n_mem = mem.shape[1]
    depth = w_in.shape[0]
    gdn_heads = gdn_a_log.shape[1]
    gdn_hd = gdn_norm_g.shape[1]
    gdn_w = gdn_heads * gdn_hd
    gla_kw = gla_b_gate.shape[1]
    gla_dv = gla_norm_g.shape[1]
    gla_vw = w_branch_gla.shape[1]
    gla_heads = gla_vw // gla_dv
    gla_dk = gla_kw // gla_heads
    rank = gla_w_gate2.shape[1]
    assert depth >= 1 and 2 * gdn_heads + rank <= LANES and seq % CHUNK == 0

    sizes = (gdn_w, gdn_w, gdn_w, gdn_w, gdn_heads, gdn_heads, gla_kw, gla_kw, gla_vw, gla_vw, rank, d, d)
    offs = [0]
    for s in sizes:
        offs.append(offs[-1] + s)
    assert offs[-1] == w_in.shape[2]
    big_cols = [0, 1, 2, 3, 6, 7, 8, 9, 11, 12]
    gla_col0 = 4 * gdn_w
    gate_col = gla_col0 + 2 * gla_kw + 2 * gla_vw

    t = bsz * seq
    x2 = x.reshape(t, d).astype(F32)
    mem2 = mem.reshape(bsz * n_mem, d).astype(F32)
    row = lambda v: v.reshape(1, -1).astype(F32)
    tm = _tile(t, 1024)
    ts = _tile(seq, 256)

    for i in range(depth):
        wi = w_in[i]
        w_big = jnp.concatenate([wi[:, offs[c]:offs[c + 1]] for c in big_cols], axis=1).astype(BF16)
        w_small = jnp.zeros((d, LANES), F32)
        w_small = w_small.at[:, 0:gdn_heads].set(wi[:, offs[4]:offs[5]])
        w_small = w_small.at[:, gdn_heads:2 * gdn_heads].set(wi[:, offs[5]:offs[6]])
        w_small = w_small.at[:, 2 * gdn_heads:2 * gdn_heads + rank].set(wi[:, offs[10]:offs[11]])
        w_gate = jnp.zeros((LANES, gla_kw), F32).at[2 * gdn_heads:2 * gdn_heads + rank].set(gla_w_gate2[i])

        proj, small = _inproj(x2, row(norm_mix_g[i]), w_big, w_small.astype(BF16), tm=tm, tn=_tile(w_big.shape[1], 1024))
        o_a = _gdn(proj, small, gdn_conv_w[i].astype(F32), _pad_lanes(gdn_a_log[i], 0),
                   _pad_lanes(gdn_dt_bias[i], 0), row(gdn_norm_g[i]),
                   bsz=bsz, seq=seq, ts=ts, nh=gdn_heads, hd=gdn_hd)
        o_b = _gla(proj, small, w_gate.astype(BF16), row(gla_b_gate[i]), row(gla_norm_g[i]),
                   bsz=bsz, seq=seq, ts=ts, nh=gla_heads, dk=gla_dk, dv=gla_dv, col0=gla_col0)
        x2 = _merge(x2, o_a, o_b, proj, w_branch_gdn[i].astype(BF16), w_branch_gla[i].astype(BF16),
                    w_out[i].astype(BF16), tm=_tile(t, 512), gate_col=gate_col)

        k_mem, v_mem = _memkv(mem2, row(norm_mem_g[i]), xattn_wk[i].astype(BF16), xattn_wv[i].astype(BF16), n_mem=n_mem)
        x2 = _xattn(x2, row(norm_xattn_g[i]), k_mem, v_mem, xattn_wq[i].astype(BF16), xattn_wo[i].astype(BF16),
                    bsz=bsz, seq=seq, tm=_tile(seq, 512), n_mem=n_mem)

        last = i == depth - 1
        x2 = _mlp(x2, row(norm_mlp_g[i]), mlp_w1[i].astype(BF16), mlp_w2[i].astype(BF16), row(norm_final_g),
                  tm=tm, tf=_tile(mlp_w1.shape[2], 1024), final_norm=last)
    return x2.reshape(bsz, seq, d).astype(x.dtype)
```

```python
import functools

import jax
import jax.numpy as jnp
from jax import lax
from jax.experimental import pallas as pl
from jax.experimental.pallas import tpu as pltpu

F32 = jnp.float32
BF16 = jnp.bfloat16

CHUNK = 64
CONV_TAPS = 4
GLA_GATE_TAU = 16.0
XATTN_HEADS = 4
NORM_EPS = 1e-6
LANES = 128
CARRY_ROWS = 8
INV_LEAF = 16
PACK = 256

VMEM_LIMIT = 48 * 1024 * 1024


def _params(semantics):
    return pltpu.CompilerParams(dimension_semantics=semantics, vmem_limit_bytes=VMEM_LIMIT)


def _dot(a, b):
    return jnp.dot(a.astype(BF16), b.astype(BF16), preferred_element_type=F32)


def _dot_nt(a, b):
    return lax.dot_general(a.astype(BF16), b.astype(BF16), (((1,), (1,)), ((), ())),
                           preferred_element_type=F32)


def _split(a):
    hi = a.astype(BF16)
    lo = (a - hi.astype(F32)).astype(BF16)
    return hi, lo


def _dot_x3(a, b):
    ah, al = _split(a)
    bh, bl = _split(b)
    d = functools.partial(jnp.dot, preferred_element_type=F32)
    return d(ah, bh) + (d(ah, bl) + d(al, bh))


def _dot_exact_lhs(tri_bf16, b):
    bh, bl = _split(b)
    d = functools.partial(jnp.dot, preferred_element_type=F32)
    return d(tri_bf16, bh) + d(tri_bf16, bl)


def _sigmoid(x):
    return 1.0 / (1.0 + jnp.exp(-x))


def _softplus(x):
    return jnp.maximum(x, 0.0) + jnp.log1p(jnp.exp(-jnp.abs(x)))


def _rms(x, g):
    return x * lax.rsqrt(jnp.mean(x * x, axis=-1, keepdims=True) + NORM_EPS) * g


def _chunk_cumsum_matrix(ts):
    r = lax.broadcasted_iota(jnp.int32, (ts, ts), 0)
    c = lax.broadcasted_iota(jnp.int32, (ts, ts), 1)
    return jnp.where((r >= c) & ((r // CHUNK) == (c // CHUNK)), 1.0, 0.0).astype(BF16)


def _inproj_kernel(x_ref, g_ref, wbig_ref, wsm_ref, big_ref, sm_ref, h_scr):
    @pl.when(pl.program_id(1) == 0)
    def _():
        h = _rms(x_ref[...], g_ref[...]).astype(BF16)
        h_scr[...] = h
        sm_ref[...] = jnp.dot(h, wsm_ref[...], preferred_element_type=F32)

    big_ref[...] = jnp.dot(h_scr[...], wbig_ref[...], preferred_element_type=F32).astype(big_ref.dtype)


def _inproj(x2, g, w_big, w_small, *, tm, tn):
    t, d = x2.shape
    n = w_big.shape[1]
    return pl.pallas_call(
        _inproj_kernel,
        grid=(t // tm, n // tn),
        in_specs=[
            pl.BlockSpec((tm, d), lambda i, j: (i, 0)),
            pl.BlockSpec((1, d), lambda i, j: (0, 0)),
            pl.BlockSpec((d, tn), lambda i, j: (0, j)),
            pl.BlockSpec((d, LANES), lambda i, j: (0, 0)),
        ],
        out_specs=[
            pl.BlockSpec((tm, tn), lambda i, j: (i, j)),
            pl.BlockSpec((tm, LANES), lambda i, j: (i, 0)),
        ],
        out_shape=[
            jax.ShapeDtypeStruct((t, n), BF16),
            jax.ShapeDtypeStruct((t, LANES), F32),
        ],
        scratch_shapes=[pltpu.VMEM((tm, d), BF16)],
        compiler_params=_params(("parallel", "arbitrary")),
        name="inproj",
    )(x2, g, w_big, w_small)


def _mask01(shape, fn):
    r = lax.broadcasted_iota(jnp.int32, shape, 0)
    c = lax.broadcasted_iota(jnp.int32, shape, 1)
    return fn(r, c)


def _bf01(shape, fn):
    return jnp.where(_mask01(shape, fn), 1.0, 0.0).astype(BF16)


def _dot_exact_rhs(a, e_bf16):
    ah, al = _split(a)
    d = functools.partial(jnp.dot, preferred_element_type=F32)
    return d(ah, e_bf16) + d(al, e_bf16)


def _mm_packed(lhs, bp, bd01):
    reps = PACK // CHUNK
    lh, ll = _split(lhs)
    bh, bl = _split(bp)
    bdh = jnp.concatenate([bh] * reps, axis=0) * bd01
    bdl = jnp.concatenate([bl] * reps, axis=0) * bd01
    d = functools.partial(jnp.dot, preferred_element_type=F32)
    return d(lh, bdh) + (d(lh, bdl) + d(ll, bdh))


def _packed_unit_lower_inverse(lows, eye_p, leaf_p, bd01):
    mm = functools.partial(_mm_packed, bd01=bd01)
    stack = lambda a, b: jnp.concatenate([a, b], axis=0)
    dg = [jnp.where(leaf_p, l, 0.0) for l in lows]
    off = [l - d for l, d in zip(lows, dg)]
    d2 = [mm(d, d) for d in dg]
    p = [eye_p - d for d in dg]
    r = [mm(stack(pi, di), di) for pi, di in zip(p, d2)]
    p = [pi + ri[:CHUNK] for pi, ri in zip(p, r)]
    d4 = [ri[CHUNK:] for ri in r]
    r = [mm(stack(pi, di), di) for pi, di in zip(p, d4)]
    p = [pi + ri[:CHUNK] for pi, ri in zip(p, r)]
    d8 = [ri[CHUNK:] for ri in r]
    p = [pi + mm(pi, di) for pi, di in zip(p, d8)]
    n = [mm(pi, oi) for pi, oi in zip(p, off)]
    n2 = [mm(ni, ni) for ni in n]
    rr = [eye_p - ni for ni in n]
    rr = [ri + mm(ri, ni) for ri, ni in zip(rr, n2)]
    return [mm(ri, pi) for ri, pi in zip(rr, p)]


def _gdn_kernel(q_ref, k_ref, v_ref, z_ref, sm_ref, cw_ref, alog_ref, dtb_ref, ng_ref, o_ref,
                xbuf, qn_scr, kn_scr, vv_scr, gcw_scr, btw_scr, w2_scr, bm_scr, qp_scr, au_scr, ee_scr,
                s_scr, *, ts, nh, hd):
    w = nh * hd
    nch = ts // CHUNK
    gh = PACK // CHUNK
    ngrp = nh // gh
    ph = PACK // hd
    npair = nh // ph

    @pl.when(pl.program_id(1) == 0)
    def _():
        xbuf[0:CARRY_ROWS, :] = jnp.zeros((CARRY_ROWS, 3 * w), F32)
        s_scr[...] = jnp.zeros_like(s_scr)

    for part, (src, dst) in enumerate(((q_ref, qn_scr), (k_ref, kn_scr), (v_ref, vv_scr))):
        xbuf[CARRY_ROWS:CARRY_ROWS + ts, part * w:(part + 1) * w] = src[...].astype(F32)
        for h in range(nh):
            cs = slice(part * w + h * hd, part * w + (h + 1) * hd)
            y = None
            for i in range(CONV_TAPS):
                r0 = CARRY_ROWS - (CONV_TAPS - 1) + i
                term = cw_ref[i:i + 1, cs] * xbuf[r0:r0 + ts, cs]
                y = term if y is None else y + term
            y = y * _sigmoid(y)
            if part < 2:
                y = y * lax.rsqrt(jnp.sum(y * y, axis=-1, keepdims=True) + NORM_EPS)
            if part == 0:
                y = y * (hd ** -0.5)
            dst[:, h * hd:(h + 1) * hd] = y
    xbuf[0:CARRY_ROWS, :] = xbuf[ts:ts + CARRY_ROWS, :]

    sm = sm_ref[...]
    g = -jnp.exp(alog_ref[...]) * _softplus(sm + dtb_ref[...])
    gc = _dot_exact_lhs(_chunk_cumsum_matrix(ts), g)
    beta = _sigmoid(sm)
    gcw_scr[...] = _dot_exact_rhs(gc, _bf01((LANES, w), lambda r, c: r == c // hd))
    btw_scr[...] = _dot_exact_rhs(beta, _bf01((LANES, w), lambda r, c: r == nh + c // hd))

    tile_p = (ts, PACK)
    incl_p = _mask01(tile_p, lambda r, c: r % CHUNK >= c % CHUNK)
    diag_p = _mask01(tile_p, lambda r, c: r % CHUNK == c % CHUNK)
    same_chunk = _bf01((ts, ts), lambda r, c: r // CHUNK == c // CHUNK)
    chunk_p = (CHUNK, PACK)
    strict_p = _mask01(chunk_p, lambda r, c: r > c % CHUNK)
    leaf_p = _mask01(chunk_p, lambda r, c: r // INV_LEAF == (c % CHUNK) // INV_LEAF)
    eye_p = jnp.where(_mask01(chunk_p, lambda r, c: r == c % CHUNK), 1.0, 0.0).astype(F32)
    head_p = [_mask01(chunk_p, lambda r, c, hh=hh: c // CHUNK == hh) for hh in range(gh)]
    bd01 = _bf01((PACK, PACK), lambda r, c: r // CHUNK == c // CHUNK)
    kbd01 = _bf01((PACK, gh * hd), lambda r, c: r // CHUNK == c // hd)

    dec_g, bcol_g = [], []
    for gi in range(ngrp):
        gcol = _dot_exact_rhs(gc, _bf01((LANES, PACK), lambda r, c, gi=gi: r == gi * gh + c // CHUNK))
        grow = _dot_exact_lhs(same_chunk, jnp.where(diag_p, gcol, 0.0))
        dec_g.append(jnp.exp(jnp.where(incl_p, gcol - grow, -jnp.inf)))
        bcol_g.append(_dot_exact_rhs(beta, _bf01((LANES, PACK), lambda r, c, gi=gi: r == nh + gi * gh + c // CHUNK)))

    chains = [(c, gi) for c in range(nch) for gi in range(ngrp)]
    lows, a_ps = [], []
    for c, gi in chains:
        rows = slice(c * CHUNK, (c + 1) * CHUNK)
        gcols = slice(gi * gh * hd, (gi + 1) * gh * hd)
        qg = qn_scr[rows, gcols].astype(BF16)
        kg = kn_scr[rows, gcols].astype(BF16)
        kbd = jnp.concatenate([kg] * gh, axis=0) * kbd01
        qk_kk = lax.dot_general(jnp.concatenate([qg, kg], axis=0), kbd, (((1,), (1,)), ((), ())),
                                preferred_element_type=F32)
        dec = dec_g[gi][rows]
        lows.append(jnp.where(strict_p, bcol_g[gi][rows] * qk_kk[CHUNK:] * dec, 0.0))
        a_ps.append(qk_kk[:CHUNK] * dec)

    tinvs = _packed_unit_lower_inverse(lows, eye_p, leaf_p, bd01)

    def blockrows(mat_p):
        return jnp.concatenate([jnp.where(m, mat_p, 0.0) for m in head_p], axis=0).astype(BF16)

    uws = []
    for (c, gi), tinv in zip(chains, tinvs):
        rows = slice(c * CHUNK, (c + 1) * CHUNK)
        rhs = []
        for hh in range(gh):
            hc = slice((gi * gh + hh) * hd, (gi * gh + hh + 1) * hd)
            bt = btw_scr[rows, hc]
            rhs.append(jnp.concatenate([bt * vv_scr[rows, hc],
                                        bt * jnp.exp(gcw_scr[rows, hc]) * kn_scr[rows, hc]], axis=1))
        uws.append(_dot(blockrows(tinv), jnp.concatenate(rhs, axis=0)))
    auws = [_dot(blockrows(a_p), uw) for a_p, uw in zip(a_ps, uws)]

    for (c, gi), uw, auw in zip(chains, uws, auws):
        rows = slice(c * CHUNK, (c + 1) * CHUNK)
        last = slice((c + 1) * CHUNK - 1, (c + 1) * CHUNK)
        for hh in range(gh):
            h = gi * gh + hh
            hc = slice(h * hd, (h + 1) * hd)
            hr = slice(hh * CHUNK, (hh + 1) * CHUNK)
            gcb = gcw_scr[rows, hc]
            au_scr[rows, hc] = auw[hr, :hd]
            qp_scr[rows, hc] = (qn_scr[rows, hc] * jnp.exp(gcb) - auw[hr, hd:]).astype(qp_scr.dtype)
            kd_t = (kn_scr[rows, hc] * jnp.exp(gcw_scr[last, hc] - gcb)).T
            bw = _dot(kd_t, uw[hr])
            sc = slice((h % ph) * hd, (h % ph + 1) * hd)
            bm_scr[c, h // ph, :, sc] = bw[:, :hd]
            w2_scr[c, h // ph, :, sc] = bw[:, hd:].astype(w2_scr.dtype)
    for c in range(nch):
        last = slice((c + 1) * CHUNK - 1, (c + 1) * CHUNK)
        ee_scr[c * CARRY_ROWS:(c + 1) * CARRY_ROWS, :] = jnp.broadcast_to(jnp.exp(gcw_scr[last, :]), (CARRY_ROWS, w))

    left = _mask01((hd, PACK), lambda r, c: c < hd)
    ng = ng_ref[...]

    def chunk_body(c, carry):
        rows = pl.ds(pl.multiple_of(c * CHUNK, CHUNK), CHUNK)
        ee = ee_scr[pl.ds(pl.multiple_of(c * CARRY_ROWS, CARRY_ROWS), 1), :]
        prods = []
        for p in range(npair):
            pc = slice(p * PACK, (p + 1) * PACK)
            s = s_scr[p]
            s_bd = jnp.concatenate([jnp.where(left, s, 0.0), jnp.where(left, 0.0, s)], axis=0).astype(BF16)
            lhs = jnp.concatenate([w2_scr[c, p], qp_scr[rows, pc]], axis=0)
            prods.append((s, jnp.dot(lhs, s_bd, preferred_element_type=F32)))
        for p, (s, r) in enumerate(prods):
            pc = slice(p * PACK, (p + 1) * PACK)
            s_scr[p] = ee[:, pc] * s + bm_scr[c, p] - r[:hd]
            o = au_scr[rows, pc] + r[hd:]
            for side in range(ph):
                hc = slice((p * ph + side) * hd, (p * ph + side + 1) * hd)
                z = z_ref[rows, hc].astype(F32)
                o_ref[rows, hc] = (_rms(o[:, side * hd:(side + 1) * hd], ng) * (z * _sigmoid(z))).astype(o_ref.dtype)
        return carry

    lax.fori_loop(0, nch, chunk_body, 0)


def _gdn(proj, small, conv_w, alog, dtb, norm_g, *, bsz, seq, ts, nh, hd):
    t = bsz * seq
    nt = seq // ts
    w = nh * hd
    nch = ts // CHUNK
    assert PACK % CHUNK == 0 and PACK % hd == 0 and nh % (PACK // CHUNK) == 0 and nh % (PACK // hd) == 0
    tok = lambda cb: pl.BlockSpec((ts, w), lambda b, j: (b * nt + j, cb))
    full = lambda shape: pl.BlockSpec(shape, lambda b, j: (0, 0))
    return pl.pallas_call(
        functools.partial(_gdn_kernel, ts=ts, nh=nh, hd=hd),
        grid=(bsz, nt),
        in_specs=[tok(0), tok(1), tok(2), tok(3),
                  pl.BlockSpec((ts, LANES), lambda b, j: (b * nt + j, 0)),
                  full((CONV_TAPS, 3 * w)), full((1, LANES)), full((1, LANES)), full((1, hd))],
        out_specs=pl.BlockSpec((ts, w), lambda b, j: (b * nt + j, 0)),
        out_shape=jax.ShapeDtypeStruct((t, w), BF16),
        scratch_shapes=[
            pltpu.VMEM((CARRY_ROWS + ts, 3 * w), F32),
            pltpu.VMEM((ts, w), F32), pltpu.VMEM((ts, w), F32), pltpu.VMEM((ts, w), F32),
            pltpu.VMEM((ts, w), F32), pltpu.VMEM((ts, w), F32),
            pltpu.VMEM((nch, nh * hd // PACK, hd, PACK), BF16),
            pltpu.VMEM((nch, nh * hd // PACK, hd, PACK), F32),
            pltpu.VMEM((ts, w), BF16), pltpu.VMEM((ts, w), F32),
            pltpu.VMEM((nch * CARRY_ROWS, w), F32),
            pltpu.VMEM((nh * hd // PACK, hd, PACK), F32),
        ],
        compiler_params=_params(("parallel", "arbitrary")),
        name="gdn",
    )(proj, proj, proj, proj, small, conv_w, alog, dtb, norm_g)


def _gla_kernel(q_ref, k_ref, v_ref, r_ref, sm_ref, wg_ref, bg_ref, ng_ref, o_ref,
                bc_scr, st_scr, *, ts, nh, dk, dv):
    nch = ts // CHUNK

    @pl.when(pl.program_id(1) == 0)
    def _():
        st_scr[...] = jnp.zeros_like(st_scr)

    lg = _dot(sm_ref[...], wg_ref[...]) + bg_ref[...]
    log_fg = (jnp.minimum(lg, 0.0) - jnp.log1p(jnp.exp(-jnp.abs(lg)))) / GLA_GATE_TAU
    bc_scr[...] = _dot_exact_lhs(_chunk_cumsum_matrix(ts), log_fg)

    row = lax.broadcasted_iota(jnp.int32, (CHUNK, CHUNK), 0)
    col = lax.broadcasted_iota(jnp.int32, (CHUNK, CHUNK), 1)
    incl = row >= col
    ng = ng_ref[...]
    scale = dk ** -0.5

    def chunk_body(c, carry):
        r0 = pl.multiple_of(c * CHUNK, CHUNK)
        rows = pl.ds(r0, CHUNK)
        bc = bc_scr[rows, :]
        bref = bc[CHUNK // 2:CHUNK // 2 + 1, :]
        bend = bc[CHUNK - 1:CHUNK, :]
        q = q_ref[rows, :].astype(F32) * scale
        k = k_ref[rows, :].astype(F32)
        qe = q * jnp.exp(bc - bref)
        ke = k * jnp.exp(bref - bc)
        qd = q * jnp.exp(bc)
        kd = k * jnp.exp(bend - bc)
        ge = jnp.exp(bend)
        for h in range(nh):
            ks = slice(h * dk, (h + 1) * dk)
            vs = slice(h * dv, (h + 1) * dv)
            a = jnp.where(incl, _dot_nt(qe[:, ks], ke[:, ks]), 0.0)
            v = v_ref[rows, vs].astype(F32)
            st = st_scr[h]
            o = _dot_nt(qd[:, ks], st) + _dot(a, v)
            st_scr[h] = ge[:, ks] * st + _dot(v.T, kd[:, ks])
            r = r_ref[rows, vs].astype(F32)
            o_ref[rows, vs] = (_rms(o, ng) * (r * _sigmoid(r))).astype(o_ref.dtype)
        return carry

    lax.fori_loop(0, nch, chunk_body, 0)


def _gla(proj, small, w_gate, b_gate, norm_g, *, bsz, seq, ts, nh, dk, dv, col0):
    t = bsz * seq
    nt = seq // ts
    kw, vw = nh * dk, nh * dv
    spec = lambda width, start: pl.BlockSpec((ts, width), lambda b, j: (b * nt + j, start // width))
    full = lambda shape: pl.BlockSpec(shape, lambda b, j: (0, 0))
    return pl.pallas_call(
        functools.partial(_gla_kernel, ts=ts, nh=nh, dk=dk, dv=dv),
        grid=(bsz, nt),
        in_specs=[spec(kw, col0), spec(kw, col0 + kw), spec(vw, col0 + 2 * kw), spec(vw, col0 + 2 * kw + vw),
                  pl.BlockSpec((ts, LANES), lambda b, j: (b * nt + j, 0)),
                  full((LANES, kw)), full((1, kw)), full((1, dv))],
        out_specs=pl.BlockSpec((ts, vw), lambda b, j: (b * nt + j, 0)),
        out_shape=jax.ShapeDtypeStruct((t, vw), BF16),
        scratch_shapes=[pltpu.VMEM((ts, kw), F32), pltpu.VMEM((nh, dv, dk), F32)],
        compiler_params=_params(("parallel", "arbitrary")),
        name="gla",
    )(proj, proj, proj, proj, small, w_gate, b_gate, norm_g)


def _merge_kernel(x_ref, oa_ref, ob_ref, ga_ref, gb_ref, wa_ref, wb_ref, wo_ref, out_ref):
    ya = jnp.dot(oa_ref[...], wa_ref[...], preferred_element_type=F32)
    yb = jnp.dot(ob_ref[...], wb_ref[...], preferred_element_type=F32)
    merged = _sigmoid(ga_ref[...].astype(F32)) * ya + _sigmoid(gb_ref[...].astype(F32)) * yb
    out_ref[...] = x_ref[...] + _dot(merged, wo_ref[...])


def _merge(x2, o_a, o_b, proj, w_a, w_b, w_o, *, tm, gate_col):
    t, d = x2.shape
    tok = lambda cb: pl.BlockSpec((tm, d), lambda i: (i, cb))
    wsp = pl.BlockSpec((d, d), lambda i: (0, 0))
    return pl.pallas_call(
        _merge_kernel,
        grid=(t // tm,),
        in_specs=[tok(0), tok(0), tok(0), tok(gate_col // d), tok(gate_col // d + 1), wsp, wsp, wsp],
        out_specs=tok(0),
        out_shape=jax.ShapeDtypeStruct((t, d), F32),
        compiler_params=_params(("parallel",)),
        name="merge",
    )(x2, o_a, o_b, proj, proj, w_a, w_b, w_o)


def _memkv_kernel(m_ref, g_ref, wk_ref, wv_ref, k_ref, v_ref):
    m = _rms(m_ref[...], g_ref[...]).astype(BF16)
    k_ref[...] = jnp.dot(m, wk_ref[...], preferred_element_type=F32).astype(BF16)
    v_ref[...] = jnp.dot(m, wv_ref[...], preferred_element_type=F32).astype(BF16)


def _memkv(mem2, g, wk, wv, *, n_mem):
    t, d = mem2.shape
    tok = pl.BlockSpec((n_mem, d), lambda b: (b, 0))
    wsp = pl.BlockSpec((d, d), lambda b: (0, 0))
    return pl.pallas_call(
        _memkv_kernel,
        grid=(t // n_mem,),
        in_specs=[tok, pl.BlockSpec((1, d), lambda b: (0, 0)), wsp, wsp],
        out_specs=[tok, tok],
        out_shape=[jax.ShapeDtypeStruct((t, d), BF16)] * 2,
        compiler_params=_params(("parallel",)),
        name="memkv",
    )(mem2, g, wk, wv)


def _xattn_kernel(x_ref, g_ref, k_ref, v_ref, wq_ref, wo_ref, out_ref, *, nh):
    x = x_ref[...]
    d = x.shape[-1]
    hd = d // nh
    q = _dot(_rms(x, g_ref[...]), wq_ref[...])
    heads = []
    for h in range(nh):
        cs = slice(h * hd, (h + 1) * hd)
        s = _dot_nt(q[:, cs], k_ref[:, cs]) * (hd ** -0.5)
        e = jnp.exp(s - jnp.max(s, axis=-1, keepdims=True))
        p = e / jnp.sum(e, axis=-1, keepdims=True)
        heads.append(_dot(p, v_ref[:, cs]))
    o = jnp.concatenate(heads, axis=1)
    out_ref[...] = x + _dot(o, wo_ref[...])


def _xattn(x2, g, k_mem, v_mem, wq, wo, *, bsz, seq, tm, n_mem):
    t, d = x2.shape
    nt = seq // tm
    tok = pl.BlockSpec((tm, d), lambda b, j: (b * nt + j, 0))
    mem = pl.BlockSpec((n_mem, d), lambda b, j: (b, 0))
    wsp = pl.BlockSpec((d, d), lambda b, j: (0, 0))
    return pl.pallas_call(
        functools.partial(_xattn_kernel, nh=XATTN_HEADS),
        grid=(bsz, nt),
        in_specs=[tok, pl.BlockSpec((1, d), lambda b, j: (0, 0)), mem, mem, wsp, wsp],
        out_specs=tok,
        out_shape=jax.ShapeDtypeStruct((t, d), F32),
        compiler_params=_params(("parallel", "parallel")),
        name="xattn",
    )(x2, g, k_mem, v_mem, wq, wo)


def _mlp_kernel(x_ref, g_ref, w1_ref, w2_ref, gf_ref, out_ref, h_scr, acc_scr, *, final_norm):
    j = pl.program_id(1)

    @pl.when(j == 0)
    def _():
        h_scr[...] = _rms(x_ref[...], g_ref[...]).astype(BF16)
        acc_scr[...] = x_ref[...]

    a = jnp.dot(h_scr[...], w1_ref[...], preferred_element_type=F32)
    acc_scr[...] += _dot(jnp.square(jnp.maximum(a, 0.0)), w2_ref[...])

    @pl.when(j == pl.num_programs(1) - 1)
    def _():
        y = acc_scr[...]
        out_ref[...] = _rms(y, gf_ref[...]) if final_norm else y


def _mlp(x2, g, w1, w2, g_final, *, tm, tf, final_norm):
    t, d = x2.shape
    ff = w1.shape[1]
    tok = pl.BlockSpec((tm, d), lambda i, j: (i, 0))
    vec = pl.BlockSpec((1, d), lambda i, j: (0, 0))
    return pl.pallas_call(
        functools.partial(_mlp_kernel, final_norm=final_norm),
        grid=(t // tm, ff // tf),
        in_specs=[tok, vec, pl.BlockSpec((d, tf), lambda i, j: (0, j)),
                  pl.BlockSpec((tf, d), lambda i, j: (j, 0)), vec],
        out_specs=tok,
        out_shape=jax.ShapeDtypeStruct((t, d), F32),
        scratch_shapes=[pltpu.VMEM((tm, d), BF16), pltpu.VMEM((tm, d), F32)],
        compiler_params=_params(("parallel", "arbitrary")),
        name="mlp",
    )(x2, g, w1, w2, g_final)


def _pad_lanes(v, offset):
    return jnp.zeros((1, LANES), F32).at[0, offset:offset + v.shape[0]].set(v.astype(F32))


def _tile(n, pref):
    while n % pref:
        pref //= 2
    return pref


def kernel(x, mem, norm_mix_g, w_in, gdn_conv_w, gdn_a_log, gdn_dt_bias, gdn_norm_g, gla_w_gate2, gla_b_gate, gla_norm_g, w_branch_gdn, w_branch_gla, w_out, norm_xattn_g, norm_mem_g, xattn_wq, xattn_wk, xattn_wv, xattn_wo, norm_mlp_g, mlp_w1, mlp_w2, norm_final_g):
    bsz, seq, d = x.shape
    n_mem = mem.shape[1]
    depth = w_in.shape[0]
    gdn_heads = gdn_a_log.shape[1]
    gdn_hd = gdn_norm_g.shape[1]
    gdn_w = gdn_heads * gdn_hd
    gla_kw = gla_b_gate.shape[1]
    gla_dv = gla_norm_g.shape[1]
    gla_vw = w_branch_gla.shape[1]
    gla_heads = gla_vw // gla_dv
    gla_dk = gla_kw // gla_heads
    rank = gla_w_gate2.shape[1]
    assert depth >= 1 and 2 * gdn_heads + rank <= LANES and seq % CHUNK == 0

    sizes = (gdn_w, gdn_w, gdn_w, gdn_w, gdn_heads, gdn_heads, gla_kw, gla_kw, gla_vw, gla_vw, rank, d, d)
    offs = [0]
    for s in sizes:
        offs.append(offs[-1] + s)
    assert offs[-1] == w_in.shape[2]
    big_cols = [0, 1, 2, 3, 6, 7, 8, 9, 11, 12]
    gla_col0 = 4 * gdn_w
    gate_col = gla_col0 + 2 * gla_kw + 2 * gla_vw

    t = bsz * seq
    x2 = x.reshape(t, d).astype(F32)
    mem2 = mem.reshape(bsz * n_mem, d).astype(F32)
    row = lambda v: v.reshape(1, -1).astype(F32)
    tm = _tile(t, 1024)
    ts = _tile(seq, 256)

    for i in range(depth):
        wi = w_in[i]
        w_big = jnp.concatenate([wi[:, offs[c]:offs[c + 1]] for c in big_cols], axis=1).astype(BF16)
        w_small = jnp.zeros((d, LANES), F32)
        w_small = w_small.at[:, 0:gdn_heads].set(wi[:, offs[4]:offs[5]])
        w_small = w_small.at[:, gdn_heads:2 * gdn_heads].set(wi[:, offs[5]:offs[6]])
        w_small = w_small.at[:, 2 * gdn_heads:2 * gdn_heads + rank].set(wi[:, offs[10]:offs[11]])
        w_gate = jnp.zeros((LANES, gla_kw), F32).at[2 * gdn_heads:2 * gdn_heads + rank].set(gla_w_gate2[i])

        proj, small = _inproj(x2, row(norm_mix_g[i]), w_big, w_small.astype(BF16), tm=tm, tn=_tile(w_big.shape[1], 1024))
        o_a = _gdn(proj, small, gdn_conv_w[i].astype(F32), _pad_lanes(gdn_a_log[i], 0),
                   _pad_lanes(gdn_dt_bias[i], 0), row(gdn_norm_g[i]),
                   bsz=bsz, seq=seq, ts=ts, nh=gdn_heads, hd=gdn_hd)
        o_b = _gla(proj, small, w_gate.astype(BF16), row(gla_b_gate[i]), row(gla_norm_g[i]),
                   bsz=bsz, seq=seq, ts=ts, nh=gla_heads, dk=gla_dk, dv=gla_dv, col0=gla_col0)
        x2 = _merge(x2, o_a, o_b, proj, w_branch_gdn[i].astype(BF16), w_branch_gla[i].astype(BF16),
                    w_out[i].astype(BF16), tm=_tile(t, 512), gate_col=gate_col)

        k_mem, v_mem = _memkv(mem2, row(norm_mem_g[i]), xattn_wk[i].astype(BF16), xattn_wv[i].astype(BF16), n_mem=n_mem)
        x2 = _xattn(x2, row(norm_xattn_g[i]), k_mem, v_mem, xattn_wq[i].astype(BF16), xattn_wo[i].astype(BF16),
                    bsz=bsz, seq=seq, tm=_tile(seq, 512), n_mem=n_mem)

        last = i == depth - 1
        x2 = _mlp(x2, row(norm_mlp_g[i]), mlp_w1[i].astype(BF16), mlp_w2[i].astype(BF16), row(norm_final_g),
                  tm=tm, tf=_tile(mlp_w1.shape[2], 1024), final_norm=last)
    return x2.reshape(bsz, seq, d).astype(x.dtype)
```

```python
import functools

import jax
import jax.numpy as jnp
from jax import lax
from jax.experimental import pallas as pl
from jax.experimental.pallas import tpu as pltpu

F32 = jnp.float32
BF16 = jnp.bfloat16

CHUNK = 64
CONV_TAPS = 4
GLA_GATE_TAU = 16.0
XATTN_HEADS = 4
NORM_EPS = 1e-6
LANES = 128
CARRY_ROWS = 8
INV_LEAF = 16
PACK = 256

VMEM_LIMIT = 48 * 1024 * 1024


def _params(semantics):
    return pltpu.CompilerParams(dimension_semantics=semantics, vmem_limit_bytes=VMEM_LIMIT)


def _dot(a, b):
    return jnp.dot(a.astype(BF16), b.astype(BF16), preferred_element_type=F32)


def _dot_nt(a, b):
    return lax.dot_general(a.astype(BF16), b.astype(BF16), (((1,), (1,)), ((), ())),
                           preferred_element_type=F32)


def _split(a):
    hi = a.astype(BF16)
    lo = (a - hi.astype(F32)).astype(BF16)
    return hi, lo


def _dot_exact_lhs(tri_bf16, b):
    bh, bl = _split(b)
    d = functools.partial(jnp.dot, preferred_element_type=F32)
    return d(tri_bf16, bh) + d(tri_bf16, bl)


def _sigmoid(x):
    return 1.0 / (1.0 + jnp.exp(-x))


def _softplus(x):
    return jnp.maximum(x, 0.0) + jnp.log(1.0 + jnp.exp(-jnp.abs(x)))


def _rms(x, g):
    return x * lax.rsqrt(jnp.mean(x * x, axis=-1, keepdims=True) + NORM_EPS) * g


def _chunk_cumsum_matrix(ts):
    r = lax.broadcasted_iota(jnp.int32, (ts, ts), 0)
    c = lax.broadcasted_iota(jnp.int32, (ts, ts), 1)
    return jnp.where((r >= c) & ((r // CHUNK) == (c // CHUNK)), 1.0, 0.0).astype(BF16)


def _inproj_kernel(x_ref, g_ref, wbig_ref, wsm_ref, big_ref, sm_ref, h_scr):
    @pl.when(pl.program_id(1) == 0)
    def _():
        h = _rms(x_ref[...], g_ref[...]).astype(BF16)
        h_scr[...] = h
        sm_ref[...] = jnp.dot(h, wsm_ref[...], preferred_element_type=F32)

    big_ref[...] = jnp.dot(h_scr[...], wbig_ref[...], preferred_element_type=F32).astype(big_ref.dtype)


def _inproj(x2, g, w_big, w_small, *, tm, tn):
    t, d = x2.shape
    n = w_big.shape[1]
    return pl.pallas_call(
        _inproj_kernel,
        grid=(t // tm, n // tn),
        in_specs=[
            pl.BlockSpec((tm, d), lambda i, j: (i, 0)),
            pl.BlockSpec((1, d), lambda i, j: (0, 0)),
            pl.BlockSpec((d, tn), lambda i, j: (0, j)),
            pl.BlockSpec((d, LANES), lambda i, j: (0, 0)),
        ],
        out_specs=[
            pl.BlockSpec((tm, tn), lambda i, j: (i, j)),
            pl.BlockSpec((tm, LANES), lambda i, j: (i, 0)),
        ],
        out_shape=[
            jax.ShapeDtypeStruct((t, n), BF16),
            jax.ShapeDtypeStruct((t, LANES), F32),
        ],
        scratch_shapes=[pltpu.VMEM((tm, d), BF16)],
        compiler_params=_params(("parallel", "arbitrary")),
        name="inproj",
    )(x2, g, w_big, w_small)


def _mask01(shape, fn):
    r = lax.broadcasted_iota(jnp.int32, shape, 0)
    c = lax.broadcasted_iota(jnp.int32, shape, 1)
    return fn(r, c)


def _bf01(shape, fn):
    return jnp.where(_mask01(shape, fn), 1.0, 0.0).astype(BF16)


def _dot_exact_rhs(a, e_bf16):
    ah, al = _split(a)
    d = functools.partial(jnp.dot, preferred_element_type=F32)
    return d(ah, e_bf16) + d(al, e_bf16)


def _mm_packed(lhs, bp, bd01):
    reps = PACK // CHUNK
    bd = jnp.concatenate([bp.astype(BF16)] * reps, axis=0) * bd01
    return jnp.dot(lhs.astype(BF16), bd, preferred_element_type=F32)


def _packed_unit_lower_inverse(lows, eye_p, leaf_p, bd01):
    mm = functools.partial(_mm_packed, bd01=bd01)
    stack = lambda a, b: jnp.concatenate([a, b], axis=0)
    dg = [jnp.where(leaf_p, l, 0.0) for l in lows]
    off = [l - d for l, d in zip(lows, dg)]
    d2 = [mm(d, d) for d in dg]
    p = [eye_p - d for d in dg]
    r = [mm(stack(pi, di), di) for pi, di in zip(p, d2)]
    p = [pi + ri[:CHUNK] for pi, ri in zip(p, r)]
    d4 = [ri[CHUNK:] for ri in r]
    r = [mm(stack(pi, di), di) for pi, di in zip(p, d4)]
    p = [pi + ri[:CHUNK] for pi, ri in zip(p, r)]
    d8 = [ri[CHUNK:] for ri in r]
    p = [pi + mm(pi, di) for pi, di in zip(p, d8)]
    n = [mm(pi, oi) for pi, oi in zip(p, off)]
    n2 = [mm(ni, ni) for ni in n]
    rr = [eye_p - ni for ni in n]
    rr = [ri + mm(ri, ni) for ri, ni in zip(rr, n2)]
    return [mm(ri, pi) for ri, pi in zip(rr, p)]


def _gdn_kernel(q_ref, k_ref, v_ref, z_ref, sm_ref, cw_ref, alog_ref, dtb_ref, ng_ref, o_ref,
                xbuf, qn_scr, kn_scr, vv_scr, gcw_scr, btw_scr, w2_scr, bm_scr, qp_scr, au_scr, ee_scr,
                s_scr, *, ts, nh, hd):
    w = nh * hd
    nch = ts // CHUNK
    gh = PACK // CHUNK
    ngrp = nh // gh
    ph = PACK // hd
    npair = nh // ph

    @pl.when(pl.program_id(1) == 0)
    def _():
        xbuf[0:CARRY_ROWS, :] = jnp.zeros((CARRY_ROWS, 3 * w), F32)
        s_scr[...] = jnp.zeros_like(s_scr)

    for part, (src, dst) in enumerate(((q_ref, qn_scr), (k_ref, kn_scr), (v_ref, vv_scr))):
        xbuf[CARRY_ROWS:CARRY_ROWS + ts, part * w:(part + 1) * w] = src[...].astype(F32)
        for h in range(nh):
            cs = slice(part * w + h * hd, part * w + (h + 1) * hd)
            y = None
            for i in range(CONV_TAPS):
                r0 = CARRY_ROWS - (CONV_TAPS - 1) + i
                term = cw_ref[i:i + 1, cs] * xbuf[r0:r0 + ts, cs]
                y = term if y is None else y + term
            y = y * _sigmoid(y)
            if part < 2:
                y = y * lax.rsqrt(jnp.sum(y * y, axis=-1, keepdims=True) + NORM_EPS)
            if part == 0:
                y = y * (hd ** -0.5)
            dst[:, h * hd:(h + 1) * hd] = y
    xbuf[0:CARRY_ROWS, :] = xbuf[ts:ts + CARRY_ROWS, :]

    sm = sm_ref[...]
    g = -jnp.exp(alog_ref[...]) * _softplus(sm + dtb_ref[...])
    gc = _dot_exact_lhs(_chunk_cumsum_matrix(ts), g)
    beta = _sigmoid(sm)
    gcw_scr[...] = _dot_exact_rhs(gc, _bf01((LANES, w), lambda r, c: r == c // hd))
    btw_scr[...] = _dot_exact_rhs(beta, _bf01((LANES, w), lambda r, c: r == nh + c // hd))

    tile_p = (ts, PACK)
    incl_p = _mask01(tile_p, lambda r, c: r % CHUNK >= c % CHUNK)
    diag_p = _mask01(tile_p, lambda r, c: r % CHUNK == c % CHUNK)
    same_chunk = _bf01((ts, ts), lambda r, c: r // CHUNK == c // CHUNK)
    chunk_p = (CHUNK, PACK)
    strict_p = _mask01(chunk_p, lambda r, c: r > c % CHUNK)
    leaf_p = _mask01(chunk_p, lambda r, c: r // INV_LEAF == (c % CHUNK) // INV_LEAF)
    eye_p = jnp.where(_mask01(chunk_p, lambda r, c: r == c % CHUNK), 1.0, 0.0).astype(F32)
    head_p = [_mask01(chunk_p, lambda r, c, hh=hh: c // CHUNK == hh) for hh in range(gh)]
    bd01 = _bf01((PACK, PACK), lambda r, c: r // CHUNK == c // CHUNK)
    kbd01 = _bf01((PACK, gh * hd), lambda r, c: r // CHUNK == c // hd)

    dec_g, bcol_g = [], []
    for gi in range(ngrp):
        gcol = _dot_exact_rhs(gc, _bf01((LANES, PACK), lambda r, c, gi=gi: r == gi * gh + c // CHUNK))
        grow = _dot_exact_lhs(same_chunk, jnp.where(diag_p, gcol, 0.0))
        dec_g.append(jnp.exp(jnp.where(incl_p, gcol - grow, -jnp.inf)))
        bcol_g.append(_dot_exact_rhs(beta, _bf01((LANES, PACK), lambda r, c, gi=gi: r == nh + gi * gh + c // CHUNK)))

    chains = [(c, gi) for c in range(nch) for gi in range(ngrp)]
    lows, a_ps = [], []
    for c, gi in chains:
        rows = slice(c * CHUNK, (c + 1) * CHUNK)
        gcols = slice(gi * gh * hd, (gi + 1) * gh * hd)
        qg = qn_scr[rows, gcols].astype(BF16)
        kg = kn_scr[rows, gcols].astype(BF16)
        kbd = jnp.concatenate([kg] * gh, axis=0) * kbd01
        qk_kk = lax.dot_general(jnp.concatenate([qg, kg], axis=0), kbd, (((1,), (1,)), ((), ())),
                                preferred_element_type=F32)
        dec = dec_g[gi][rows]
        lows.append(jnp.where(strict_p, bcol_g[gi][rows] * qk_kk[CHUNK:] * dec, 0.0))
        a_ps.append(qk_kk[:CHUNK] * dec)

    tinvs = _packed_unit_lower_inverse(lows, eye_p, leaf_p, bd01)

    def blockrows(mat_p):
        return jnp.concatenate([jnp.where(m, mat_p, 0.0) for m in head_p], axis=0).astype(BF16)

    uws = []
    for (c, gi), tinv in zip(chains, tinvs):
        rows = slice(c * CHUNK, (c + 1) * CHUNK)
        rhs = []
        for hh in range(gh):
            hc = slice((gi * gh + hh) * hd, (gi * gh + hh + 1) * hd)
            bt = btw_scr[rows, hc]
            rhs.append(jnp.concatenate([bt * vv_scr[rows, hc],
                                        bt * jnp.exp(gcw_scr[rows, hc]) * kn_scr[rows, hc]], axis=1))
        uws.append(_dot(blockrows(tinv), jnp.concatenate(rhs, axis=0)))
    auws = [_dot(blockrows(a_p), uw) for a_p, uw in zip(a_ps, uws)]

    for (c, gi), uw, auw in zip(chains, uws, auws):
        rows = slice(c * CHUNK, (c + 1) * CHUNK)
        last = slice((c + 1) * CHUNK - 1, (c + 1) * CHUNK)
        for hh in range(gh):
            h = gi * gh + hh
            hc = slice(h * hd, (h + 1) * hd)
            hr = slice(hh * CHUNK, (hh + 1) * CHUNK)
            gcb = gcw_scr[rows, hc]
            au_scr[rows, hc] = auw[hr, :hd]
            qp_scr[rows, hc] = (qn_scr[rows, hc] * jnp.exp(gcb) - auw[hr, hd:]).astype(qp_scr.dtype)
            kd_t = (kn_scr[rows, hc] * jnp.exp(gcw_scr[last, hc] - gcb)).T
            bw = _dot(kd_t, uw[hr])
            sc = slice((h % ph) * hd, (h % ph + 1) * hd)
            bm_scr[c, h // ph, :, sc] = bw[:, :hd]
            w2_scr[c, h // ph, :, sc] = bw[:, hd:].astype(w2_scr.dtype)
    for c in range(nch):
        last = slice((c + 1) * CHUNK - 1, (c + 1) * CHUNK)
        ee_scr[c * CARRY_ROWS:(c + 1) * CARRY_ROWS, :] = jnp.broadcast_to(jnp.exp(gcw_scr[last, :]), (CARRY_ROWS, w))

    left = _mask01((hd, PACK), lambda r, c: c < hd)
    ng = ng_ref[...]

    def chunk_body(c, carry):
        rows = pl.ds(pl.multiple_of(c * CHUNK, CHUNK), CHUNK)
        ee = ee_scr[pl.ds(pl.multiple_of(c * CARRY_ROWS, CARRY_ROWS), 1), :]
        prods = []
        for p in range(npair):
            pc = slice(p * PACK, (p + 1) * PACK)
            s = s_scr[p]
            s_bd = jnp.concatenate([jnp.where(left, s, 0.0), jnp.where(left, 0.0, s)], axis=0).astype(BF16)
            lhs = jnp.concatenate([w2_scr[c, p], qp_scr[rows, pc]], axis=0)
            prods.append((s, jnp.dot(lhs, s_bd, preferred_element_type=F32)))
        for p, (s, r) in enumerate(prods):
            pc = slice(p * PACK, (p + 1) * PACK)
            s_scr[p] = ee[:, pc] * s + bm_scr[c, p] - r[:hd]
            o = au_scr[rows, pc] + r[hd:]
            for side in range(ph):
                hc = slice((p * ph + side) * hd, (p * ph + side + 1) * hd)
                z = z_ref[rows, hc].astype(F32)
                o_ref[rows, hc] = (_rms(o[:, side * hd:(side + 1) * hd], ng) * (z * _sigmoid(z))).astype(o_ref.dtype)
        return carry

    lax.fori_loop(0, nch, chunk_body, 0)


def _gdn(proj, small, conv_w, alog, dtb, norm_g, *, bsz, seq, ts, nh, hd):
    t = bsz * seq
    nt = seq // ts
    w = nh * hd
    nch = ts // CHUNK
    assert PACK % CHUNK == 0 and PACK % hd == 0 and nh % (PACK // CHUNK) == 0 and nh % (PACK // hd) == 0
    tok = lambda cb: pl.BlockSpec((ts, w), lambda b, j: (b * nt + j, cb))
    full = lambda shape: pl.BlockSpec(shape, lambda b, j: (0, 0))
    return pl.pallas_call(
        functools.partial(_gdn_kernel, ts=ts, nh=nh, hd=hd),
        grid=(bsz, nt),
        in_specs=[tok(0), tok(1), tok(2), tok(3),
                  pl.BlockSpec((ts, LANES), lambda b, j: (b * nt + j, 0)),
                  full((CONV_TAPS, 3 * w)), full((1, LANES)), full((1, LANES)), full((1, hd))],
        out_specs=pl.BlockSpec((ts, w), lambda b, j: (b * nt + j, 0)),
        out_shape=jax.ShapeDtypeStruct((t, w), BF16),
        scratch_shapes=[
            pltpu.VMEM((CARRY_ROWS + ts, 3 * w), F32),
            pltpu.VMEM((ts, w), F32), pltpu.VMEM((ts, w), F32), pltpu.VMEM((ts, w), F32),
            pltpu.VMEM((ts, w), F32), pltpu.VMEM((ts, w), F32),
            pltpu.VMEM((nch, nh * hd // PACK, hd, PACK), BF16),
            pltpu.VMEM((nch, nh * hd // PACK, hd, PACK), F32),
            pltpu.VMEM((ts, w), BF16), pltpu.VMEM((ts, w), F32),
            pltpu.VMEM((nch * CARRY_ROWS, w), F32),
            pltpu.VMEM((nh * hd // PACK, hd, PACK), F32),
        ],
        compiler_params=_params(("parallel", "arbitrary")),
        name="gdn",
    )(proj, proj, proj, proj, small, conv_w, alog, dtb, norm_g)


def _gla_kernel(q_ref, k_ref, v_ref, r_ref, sm_ref, wg_ref, bg_ref, ng_ref, o_ref, st_scr, *, ts, nh, dk, dv):
    nch = ts // CHUNK
    gh = PACK // CHUNK
    ngrp = nh // gh

    @pl.when(pl.program_id(1) == 0)
    def _():
        st_scr[...] = jnp.zeros_like(st_scr)

    lg = _dot(sm_ref[...], wg_ref[...]) + bg_ref[...]
    log_fg = (jnp.minimum(lg, 0.0) - jnp.log(1.0 + jnp.exp(-jnp.abs(lg)))) / GLA_GATE_TAU
    bc_all = _dot_exact_lhs(_chunk_cumsum_matrix(ts), log_fg)

    chunk_p = (CHUNK, PACK)
    incl_p = _mask01(chunk_p, lambda r, c: r >= c % CHUNK)
    head_p = [_mask01(chunk_p, lambda r, c, hh=hh: c // CHUNK == hh) for hh in range(gh)]
    kbd01 = _bf01((PACK, gh * dk), lambda r, c: r // CHUNK == c // dk)
    ng = ng_ref[...]
    scale = dk ** -0.5

    def blockrows(mat_p):
        return jnp.concatenate([jnp.where(m, mat_p, 0.0) for m in head_p], axis=0).astype(BF16)

    qd, kd, ge, a_ps = [], [], [], []
    for c in range(nch):
        rows = slice(c * CHUNK, (c + 1) * CHUNK)
        bc = bc_all[rows]
        bref = bc[CHUNK // 2:CHUNK // 2 + 1, :]
        bend = bc[CHUNK - 1:CHUNK, :]
        q = q_ref[rows, :].astype(F32) * scale
        k = k_ref[rows, :].astype(F32)
        qe = (q * jnp.exp(bc - bref)).astype(BF16)
        ke = (k * jnp.exp(bref - bc)).astype(BF16)
        qd.append((q * jnp.exp(bc)).astype(BF16))
        kd.append((k * jnp.exp(bend - bc)).astype(BF16))
        ge.append(jnp.exp(bend))
        for gi in range(ngrp):
            gcols = slice(gi * gh * dk, (gi + 1) * gh * dk)
            kebd = jnp.concatenate([ke[:, gcols]] * gh, axis=0) * kbd01
            a = lax.dot_general(qe[:, gcols], kebd, (((1,), (1,)), ((), ())), preferred_element_type=F32)
            a_ps.append(jnp.where(incl_p, a, 0.0))

    oi = []
    for c in range(nch):
        rows = slice(c * CHUNK, (c + 1) * CHUNK)
        for gi in range(ngrp):
            v4 = jnp.concatenate([v_ref[rows, (gi * gh + hh) * dv:(gi * gh + hh + 1) * dv] for hh in range(gh)], axis=0)
            oi.append(jnp.dot(blockrows(a_ps[c * ngrp + gi]), v4, preferred_element_type=F32))
    kvt = []
    for c in range(nch):
        rows = slice(c * CHUNK, (c + 1) * CHUNK)
        for h in range(nh):
            v_t = v_ref[rows, h * dv:(h + 1) * dv].astype(F32).T
            kvt.append(_dot(v_t, kd[c][:, h * dk:(h + 1) * dk]))

    st = [st_scr[h] for h in range(nh)]
    for c in range(nch):
        rows = slice(c * CHUNK, (c + 1) * CHUNK)
        for h in range(nh):
            ks = slice(h * dk, (h + 1) * dk)
            vs = slice(h * dv, (h + 1) * dv)
            hr = slice((h % gh) * CHUNK, (h % gh + 1) * CHUNK)
            o = _dot_nt(qd[c][:, ks], st[h]) + oi[c * ngrp + h // gh][hr]
            st[h] = ge[c][:, ks] * st[h] + kvt[c * nh + h]
            r = r_ref[rows, vs].astype(F32)
            o_ref[rows, vs] = (_rms(o, ng) * (r * _sigmoid(r))).astype(o_ref.dtype)
    for h in range(nh):
        st_scr[h] = st[h]


def _gla(proj, small, w_gate, b_gate, norm_g, *, bsz, seq, ts, nh, dk, dv, col0):
    t = bsz * seq
    nt = seq // ts
    kw, vw = nh * dk, nh * dv
    assert nh % (PACK // CHUNK) == 0
    spec = lambda width, start: pl.BlockSpec((ts, width), lambda b, j: (b * nt + j, start // width))
    full = lambda shape: pl.BlockSpec(shape, lambda b, j: (0, 0))
    return pl.pallas_call(
        functools.partial(_gla_kernel, ts=ts, nh=nh, dk=dk, dv=dv),
        grid=(bsz, nt),
        in_specs=[spec(kw, col0), spec(kw, col0 + kw), spec(vw, col0 + 2 * kw), spec(vw, col0 + 2 * kw + vw),
                  pl.BlockSpec((ts, LANES), lambda b, j: (b * nt + j, 0)),
                  full((LANES, kw)), full((1, kw)), full((1, dv))],
        out_specs=pl.BlockSpec((ts, vw), lambda b, j: (b * nt + j, 0)),
        out_shape=jax.ShapeDtypeStruct((t, vw), BF16),
        scratch_shapes=[pltpu.VMEM((nh, dv, dk), F32)],
        compiler_params=_params(("parallel", "arbitrary")),
        name="gla",
    )(proj, proj, proj, proj, small, w_gate, b_gate, norm_g)


def _merge_kernel(x_ref, oa_ref, ob_ref, ga_ref, gb_ref, wa_ref, wb_ref, wo_ref, out_ref):
    ya = jnp.dot(oa_ref[...], wa_ref[...], preferred_element_type=F32)
    yb = jnp.dot(ob_ref[...], wb_ref[...], preferred_element_type=F32)
    merged = _sigmoid(ga_ref[...].astype(F32)) * ya + _sigmoid(gb_ref[...].astype(F32)) * yb
    out_ref[...] = x_ref[...] + _dot(merged, wo_ref[...])


def _merge(x2, o_a, o_b, proj, w_a, w_b, w_o, *, tm, gate_col):
    t, d = x2.shape
    tok = lambda cb: pl.BlockSpec((tm, d), lambda i: (i, cb))
    wsp = pl.BlockSpec((d, d), lambda i: (0, 0))
    return pl.pallas_call(
        _merge_kernel,
        grid=(t // tm,),
        in_specs=[tok(0), tok(0), tok(0), tok(gate_col // d), tok(gate_col // d + 1), wsp, wsp, wsp],
        out_specs=tok(0),
        out_shape=jax.ShapeDtypeStruct((t, d), F32),
        compiler_params=_params(("parallel",)),
        name="merge",
    )(x2, o_a, o_b, proj, proj, w_a, w_b, w_o)


def _memkv_kernel(m_ref, g_ref, wk_ref, wv_ref, k_ref, v_ref):
    m = _rms(m_ref[...], g_ref[...]).astype(BF16)
    k_ref[...] = jnp.dot(m, wk_ref[...], preferred_element_type=F32).astype(BF16)
    v_ref[...] = jnp.dot(m, wv_ref[...], preferred_element_type=F32).astype(BF16)


def _memkv(mem2, g, wk, wv, *, n_mem):
    t, d = mem2.shape
    tok = pl.BlockSpec((n_mem, d), lambda b: (b, 0))
    wsp = pl.BlockSpec((d, d), lambda b: (0, 0))
    return pl.pallas_call(
        _memkv_kernel,
        grid=(t // n_mem,),
        in_specs=[tok, pl.BlockSpec((1, d), lambda b: (0, 0)), wsp, wsp],
        out_specs=[tok, tok],
        out_shape=[jax.ShapeDtypeStruct((t, d), BF16)] * 2,
        compiler_params=_params(("parallel",)),
        name="memkv",
    )(mem2, g, wk, wv)


def _xattn_kernel(x_ref, g_ref, k_ref, v_ref, wq_ref, wo_ref, out_ref, *, nh):
    x = x_ref[...]
    d = x.shape[-1]
    hd = d // nh
    q = _dot(_rms(x, g_ref[...]), wq_ref[...])
    heads = []
    for h in range(nh):
        cs = slice(h * hd, (h + 1) * hd)
        s = _dot_nt(q[:, cs], k_ref[:, cs]) * (hd ** -0.5)
        e = jnp.exp(s - jnp.max(s, axis=-1, keepdims=True))
        p = e / jnp.sum(e, axis=-1, keepdims=True)
        heads.append(_dot(p, v_ref[:, cs]))
    o = jnp.concatenate(heads, axis=1)
    out_ref[...] = x + _dot(o, wo_ref[...])


def _xattn(x2, g, k_mem, v_mem, wq, wo, *, bsz, seq, tm, n_mem):
    t, d = x2.shape
    nt = seq // tm
    tok = pl.BlockSpec((tm, d), lambda b, j: (b * nt + j, 0))
    mem = pl.BlockSpec((n_mem, d), lambda b, j: (b, 0))
    wsp = pl.BlockSpec((d, d), lambda b, j: (0, 0))
    return pl.pallas_call(
        functools.partial(_xattn_kernel, nh=XATTN_HEADS),
        grid=(bsz, nt),
        in_specs=[tok, pl.BlockSpec((1, d), lambda b, j: (0, 0)), mem, mem, wsp, wsp],
        out_specs=tok,
        out_shape=jax.ShapeDtypeStruct((t, d), F32),
        compiler_params=_params(("parallel", "parallel")),
        name="xattn",
    )(x2, g, k_mem, v_mem, wq, wo)


def _mlp_kernel(x_ref, g_ref, w1_ref, w2_ref, gf_ref, out_ref, h_scr, acc_scr, *, final_norm):
    j = pl.program_id(1)

    @pl.when(j == 0)
    def _():
        h_scr[...] = _rms(x_ref[...], g_ref[...]).astype(BF16)
        acc_scr[...] = x_ref[...]

    a = jnp.dot(h_scr[...], w1_ref[...], preferred_element_type=F32)
    acc_scr[...] += _dot(jnp.square(jnp.maximum(a, 0.0)), w2_ref[...])

    @pl.when(j == pl.num_programs(1) - 1)
    def _():
        y = acc_scr[...]
        out_ref[...] = _rms(y, gf_ref[...]) if final_norm else y


def _mlp(x2, g, w1, w2, g_final, *, tm, tf, final_norm):
    t, d = x2.shape
    ff = w1.shape[1]
    tok = pl.BlockSpec((tm, d), lambda i, j: (i, 0))
    vec = pl.BlockSpec((1, d), lambda i, j: (0, 0))
    return pl.pallas_call(
        functools.partial(_mlp_kernel, final_norm=final_norm),
        grid=(t // tm, ff // tf),
        in_specs=[tok, vec, pl.BlockSpec((d, tf), lambda i, j: (0, j)),
                  pl.BlockSpec((tf, d), lambda i, j: (j, 0)), vec],
        out_specs=tok,
        out_shape=jax.ShapeDtypeStruct((t, d), F32),
        scratch_shapes=[pltpu.VMEM((tm, d), BF16), pltpu.VMEM((tm, d), F32)],
        compiler_params=_params(("parallel", "arbitrary")),
        name="mlp",
    )(x2, g, w1, w2, g_final)


def _pad_lanes(v, offset):
    return jnp.zeros((1, LANES), F32).at[0, offset:offset + v.shape[0]].set(v.astype(F32))


def _tile(n, pref):
    while n % pref:
        pref //= 2
    return pref


def kernel(x, mem, norm_mix_g, w_in, gdn_conv_w, gdn_a_log, gdn_dt_bias, gdn_norm_g, gla_w_gate2, gla_b_gate, gla_norm_g, w_branch_gdn, w_branch_gla, w_out, norm_xattn_g, norm_mem_g, xattn_wq, xattn_wk, xattn_wv, xattn_wo, norm_mlp_g, mlp_w1, mlp_w2, norm_final_g):
    bsz, seq, d = x.shape
    n_mem = mem.shape[1]
    depth = w_in.shape[0]
    gdn_heads = gdn_a_log.shape[1]
    gdn_hd = gdn_norm_g.shape[1]
    gdn_w = gdn_heads * gdn_hd
    gla_kw = gla_b_gate.shape[1]
    gla_dv = gla_norm_g.shape[1]
    gla_vw = w_branch_gla.shape[1]
    gla_heads = gla_vw // gla_dv
    gla_dk = gla_kw // gla_heads
    rank = gla_w_gate2.shape[1]
    assert depth >= 1 and 2 * gdn_heads + rank <= LANES and seq % CHUNK == 0

    sizes = (gdn_w, gdn_w, gdn_w, gdn_w, gdn_heads, gdn_heads, gla_kw, gla_kw, gla_vw, gla_vw, rank, d, d)
    offs = [0]
    for s in sizes:
        offs.append(offs[-1] + s)
    assert offs[-1] == w_in.shape[2]
    big_cols = [0, 1, 2, 3, 6, 7, 8, 9, 11, 12]
    gla_col0 = 4 * gdn_w
    gate_col = gla_col0 + 2 * gla_kw + 2 * gla_vw

    t = bsz * seq
    x2 = x.reshape(t, d).astype(F32)
    mem2 = mem.reshape(bsz * n_mem, d).astype(F32)
    row = lambda v: v.reshape(1, -1).astype(F32)
    tm = _tile(t, 1024)
    ts = _tile(seq, 256)

    for i in range(depth):
        wi = w_in[i]
        w_big = jnp.concatenate([wi[:, offs[c]:offs[c + 1]] for c in big_cols], axis=1).astype(BF16)
        w_small = jnp.zeros((d, LANES), F32)
        w_small = w_small.at[:, 0:gdn_heads].set(wi[:, offs[4]:offs[5]])
        w_small = w_small.at[:, gdn_heads:2 * gdn_heads].set(wi[:, offs[5]:offs[6]])
        w_small = w_small.at[:, 2 * gdn_heads:2 * gdn_heads + rank].set(wi[:, offs[10]:offs[11]])
        w_gate = jnp.zeros((LANES, gla_kw), F32).at[2 * gdn_heads:2 * gdn_heads + rank].set(gla_w_gate2[i])

        proj, small = _inproj(x2, row(norm_mix_g[i]), w_big, w_small.astype(BF16), tm=tm, tn=_tile(w_big.shape[1], 2304))
        o_a = _gdn(proj, small, gdn_conv_w[i].astype(F32), _pad_lanes(gdn_a_log[i], 0),
                   _pad_lanes(gdn_dt_bias[i], 0), row(gdn_norm_g[i]),
                   bsz=bsz, seq=seq, ts=ts, nh=gdn_heads, hd=gdn_hd)
        o_b = _gla(proj, small, w_gate.astype(BF16), row(gla_b_gate[i]), row(gla_norm_g[i]),
                   bsz=bsz, seq=seq, ts=ts, nh=gla_heads, dk=gla_dk, dv=gla_dv, col0=gla_col0)
        x2 = _merge(x2, o_a, o_b, proj, w_branch_gdn[i].astype(BF16), w_branch_gla[i].astype(BF16),
                    w_out[i].astype(BF16), tm=_tile(t, 512), gate_col=gate_col)

        k_mem, v_mem = _memkv(mem2, row(norm_mem_g[i]), xattn_wk[i].astype(BF16), xattn_wv[i].astype(BF16), n_mem=n_mem)
        x2 = _xattn(x2, row(norm_xattn_g[i]), k_mem, v_mem, xattn_wq[i].astype(BF16), xattn_wo[i].astype(BF16),
                    bsz=bsz, seq=seq, tm=_tile(seq, 512), n_mem=n_mem)

        last = i == depth - 1
        x2 = _mlp(x2, row(norm_mlp_g[i]), mlp_w1[i].astype(BF16), mlp_w2[i].astype(BF16), row(norm_final_g),
                  tm=tm, tf=_tile(mlp_w1.shape[2], 1024), final_norm=last)
    return x2.reshape(bsz, seq, d).astype(x.dtype)
```

```python
import functools

import jax
import jax.numpy as jnp
from jax import lax
from jax.experimental import pallas as pl
from jax.experimental.pallas import tpu as pltpu

F32 = jnp.float32
BF16 = jnp.bfloat16

CHUNK = 64
CONV_TAPS = 4
GLA_GATE_TAU = 16.0
XATTN_HEADS = 4
NORM_EPS = 1e-6
NEG_LOG2E = -1.4426950408889634
LANES = 128
CARRY_ROWS = 8
INV_LEAF = 16
PACK = 256

VMEM_LIMIT = 48 * 1024 * 1024


def _params(semantics):
    return pltpu.CompilerParams(dimension_semantics=semantics, vmem_limit_bytes=VMEM_LIMIT)


def _dot(a, b):
    return jnp.dot(a.astype(BF16), b.astype(BF16), preferred_element_type=F32)


def _dot_nt(a, b):
    return lax.dot_general(a.astype(BF16), b.astype(BF16), (((1,), (1,)), ((), ())),
                           preferred_element_type=F32)


def _split(a):
    hi = a.astype(BF16)
    lo = (a - hi.astype(F32)).astype(BF16)
    return hi, lo


def _dot_exact_lhs(tri_bf16, b):
    bh, bl = _split(b)
    d = functools.partial(jnp.dot, preferred_element_type=F32)
    return d(tri_bf16, bh) + d(tri_bf16, bl)


def _sigmoid(x):
    return 1.0 / (1.0 + jnp.exp2(x * NEG_LOG2E))


def _softplus(x):
    return jnp.maximum(x, 0.0) + jnp.log(1.0 + jnp.exp(-jnp.abs(x)))


def _rms(x, g):
    return x * lax.rsqrt(jnp.mean(x * x, axis=-1, keepdims=True) + NORM_EPS) * g


def _chunk_cumsum_matrix(ts):
    r = lax.broadcasted_iota(jnp.int32, (ts, ts), 0)
    c = lax.broadcasted_iota(jnp.int32, (ts, ts), 1)
    return jnp.where((r >= c) & ((r // CHUNK) == (c // CHUNK)), 1.0, 0.0).astype(BF16)


def _inproj_kernel(x_ref, g_ref, wbig_ref, wsm_ref, big_ref, sm_ref, h_scr):
    @pl.when(pl.program_id(1) == 0)
    def _():
        h = _rms(x_ref[...], g_ref[...]).astype(BF16)
        h_scr[...] = h
        sm_ref[...] = jnp.dot(h, wsm_ref[...], preferred_element_type=F32)

    big_ref[...] = jnp.dot(h_scr[...], wbig_ref[...], preferred_element_type=F32).astype(big_ref.dtype)


def _inproj(x2, g, w_big, w_small, *, tm, tn):
    t, d = x2.shape
    n = w_big.shape[1]
    return pl.pallas_call(
        _inproj_kernel,
        grid=(t // tm, n // tn),
        in_specs=[
            pl.BlockSpec((tm, d), lambda i, j: (i, 0)),
            pl.BlockSpec((1, d), lambda i, j: (0, 0)),
            pl.BlockSpec((d, tn), lambda i, j: (0, j)),
            pl.BlockSpec((d, LANES), lambda i, j: (0, 0)),
        ],
        out_specs=[
            pl.BlockSpec((tm, tn), lambda i, j: (i, j)),
            pl.BlockSpec((tm, LANES), lambda i, j: (i, 0)),
        ],
        out_shape=[
            jax.ShapeDtypeStruct((t, n), BF16),
            jax.ShapeDtypeStruct((t, LANES), F32),
        ],
        scratch_shapes=[pltpu.VMEM((tm, d), BF16)],
        compiler_params=_params(("parallel", "arbitrary")),
        name="inproj",
    )(x2, g, w_big, w_small)


def _mask01(shape, fn):
    r = lax.broadcasted_iota(jnp.int32, shape, 0)
    c = lax.broadcasted_iota(jnp.int32, shape, 1)
    return fn(r, c)


def _bf01(shape, fn):
    return jnp.where(_mask01(shape, fn), 1.0, 0.0).astype(BF16)


def _dot_exact_rhs(a, e_bf16):
    ah, al = _split(a)
    d = functools.partial(jnp.dot, preferred_element_type=F32)
    return d(ah, e_bf16) + d(al, e_bf16)


def _mm_packed(lhs, bp, bd01):
    reps = PACK // CHUNK
    bd = jnp.concatenate([bp.astype(BF16)] * reps, axis=0) * bd01
    return jnp.dot(lhs.astype(BF16), bd, preferred_element_type=F32)


def _packed_unit_lower_inverse(lows, eye_p, leaf_p, bd01):
    mm = functools.partial(_mm_packed, bd01=bd01)
    stack = lambda a, b: jnp.concatenate([a, b], axis=0)
    dg = [jnp.where(leaf_p, l, 0.0) for l in lows]
    off = [l - d for l, d in zip(lows, dg)]
    d2 = [mm(d, d) for d in dg]
    p = [eye_p - d for d in dg]
    r = [mm(stack(pi, di), di) for pi, di in zip(p, d2)]
    p = [pi + ri[:CHUNK] for pi, ri in zip(p, r)]
    d4 = [ri[CHUNK:] for ri in r]
    r = [mm(stack(pi, di), di) for pi, di in zip(p, d4)]
    p = [pi + ri[:CHUNK] for pi, ri in zip(p, r)]
    d8 = [ri[CHUNK:] for ri in r]
    p = [pi + mm(pi, di) for pi, di in zip(p, d8)]
    n = [mm(pi, oi) for pi, oi in zip(p, off)]
    n2 = [mm(ni, ni) for ni in n]
    rr = [eye_p - ni for ni in n]
    rr = [ri + mm(ri, ni) for ri, ni in zip(rr, n2)]
    return [mm(ri, pi) for ri, pi in zip(rr, p)]


def _gdn_kernel(q_ref, k_ref, v_ref, z_ref, sm_ref, cw_ref, alog_ref, dtb_ref, ng_ref, o_ref,
                xbuf, qn_scr, kn_scr, vv_scr, gcw_scr, btw_scr, w2_scr, bm_scr, qp_scr, au_scr, ee_scr,
                s_scr, *, ts, nh, hd):
    w = nh * hd
    nch = ts // CHUNK
    gh = PACK // CHUNK
    ngrp = nh // gh
    ph = PACK // hd
    npair = nh // ph

    @pl.when(pl.program_id(1) == 0)
    def _():
        xbuf[0:CARRY_ROWS, :] = jnp.zeros((CARRY_ROWS, 3 * w), F32)
        s_scr[...] = jnp.zeros_like(s_scr)

    for part, (src, dst) in enumerate(((q_ref, qn_scr), (k_ref, kn_scr), (v_ref, vv_scr))):
        xbuf[CARRY_ROWS:CARRY_ROWS + ts, part * w:(part + 1) * w] = src[...].astype(F32)
        for h in range(nh):
            cs = slice(part * w + h * hd, part * w + (h + 1) * hd)
            y = None
            for i in range(CONV_TAPS):
                r0 = CARRY_ROWS - (CONV_TAPS - 1) + i
                term = cw_ref[i:i + 1, cs] * xbuf[r0:r0 + ts, cs]
                y = term if y is None else y + term
            y = y * _sigmoid(y)
            if part < 2:
                y = y * (lax.rsqrt(jnp.sum(y * y, axis=-1, keepdims=True) + NORM_EPS)
                         * (hd ** -0.5 if part == 0 else 1.0))
            dst[:, h * hd:(h + 1) * hd] = y
    xbuf[0:CARRY_ROWS, :] = xbuf[ts:ts + CARRY_ROWS, :]

    sm = sm_ref[...]
    g = -jnp.exp(alog_ref[...]) * _softplus(sm + dtb_ref[...])
    gc = _dot_exact_lhs(_chunk_cumsum_matrix(ts), g)
    beta = _sigmoid(sm)
    gcw_scr[...] = _dot_exact_rhs(gc, _bf01((LANES, w), lambda r, c: r == c // hd))
    btw_scr[...] = _dot_exact_rhs(beta, _bf01((LANES, w), lambda r, c: r == nh + c // hd))

    tile_p = (ts, PACK)
    incl_p = _mask01(tile_p, lambda r, c: r % CHUNK >= c % CHUNK)
    diag_p = _mask01(tile_p, lambda r, c: r % CHUNK == c % CHUNK)
    same_chunk = _bf01((ts, ts), lambda r, c: r // CHUNK == c // CHUNK)
    chunk_p = (CHUNK, PACK)
    strict_p = _mask01(chunk_p, lambda r, c: r > c % CHUNK)
    leaf_p = _mask01(chunk_p, lambda r, c: r // INV_LEAF == (c % CHUNK) // INV_LEAF)
    eye_p = jnp.where(_mask01(chunk_p, lambda r, c: r == c % CHUNK), 1.0, 0.0).astype(F32)
    head_p = [_mask01(chunk_p, lambda r, c, hh=hh: c // CHUNK == hh) for hh in range(gh)]
    bd01 = _bf01((PACK, PACK), lambda r, c: r // CHUNK == c // CHUNK)
    kbd01 = _bf01((PACK, gh * hd), lambda r, c: r // CHUNK == c // hd)

    dec_g, bcol_g = [], []
    for gi in range(ngrp):
        gcol = _dot_exact_rhs(gc, _bf01((LANES, PACK), lambda r, c, gi=gi: r == gi * gh + c // CHUNK))
        grow = _dot_exact_lhs(same_chunk, jnp.where(diag_p, gcol, 0.0))
        dec_g.append(jnp.exp(jnp.where(incl_p, gcol - grow, -jnp.inf)))
        bcol_g.append(_dot_exact_rhs(beta, _bf01((LANES, PACK), lambda r, c, gi=gi: r == nh + gi * gh + c // CHUNK)))

    chains = [(c, gi) for c in range(nch) for gi in range(ngrp)]
    lows, a_ps = [], []
    for c, gi in chains:
        rows = slice(c * CHUNK, (c + 1) * CHUNK)
        gcols = slice(gi * gh * hd, (gi + 1) * gh * hd)
        qg = qn_scr[rows, gcols].astype(BF16)
        kg = kn_scr[rows, gcols].astype(BF16)
        kbd = jnp.concatenate([kg] * gh, axis=0) * kbd01
        qk_kk = lax.dot_general(jnp.concatenate([qg, kg], axis=0), kbd, (((1,), (1,)), ((), ())),
                                preferred_element_type=F32)
        dec = dec_g[gi][rows]
        lows.append(jnp.where(strict_p, bcol_g[gi][rows] * qk_kk[CHUNK:] * dec, 0.0))
        a_ps.append(qk_kk[:CHUNK] * dec)

    tinvs = _packed_unit_lower_inverse(lows, eye_p, leaf_p, bd01)

    def blockrows(mat_p):
        return jnp.concatenate([jnp.where(m, mat_p, 0.0) for m in head_p], axis=0).astype(BF16)

    uws = []
    for (c, gi), tinv in zip(chains, tinvs):
        rows = slice(c * CHUNK, (c + 1) * CHUNK)
        rhs = []
        for hh in range(gh):
            hc = slice((gi * gh + hh) * hd, (gi * gh + hh + 1) * hd)
            bt = btw_scr[rows, hc]
            rhs.append(jnp.concatenate([bt * vv_scr[rows, hc],
                                        bt * jnp.exp(gcw_scr[rows, hc]) * kn_scr[rows, hc]], axis=1))
        uws.append(_dot(blockrows(tinv), jnp.concatenate(rhs, axis=0)))
    auws = [_dot(blockrows(a_p), uw) for a_p, uw in zip(a_ps, uws)]

    for (c, gi), uw, auw in zip(chains, uws, auws):
        rows = slice(c * CHUNK, (c + 1) * CHUNK)
        last = slice((c + 1) * CHUNK - 1, (c + 1) * CHUNK)
        for hh in range(gh):
            h = gi * gh + hh
            hc = slice(h * hd, (h + 1) * hd)
            hr = slice(hh * CHUNK, (hh + 1) * CHUNK)
            gcb = gcw_scr[rows, hc]
            au_scr[rows, hc] = auw[hr, :hd]
            qp_scr[rows, hc] = (qn_scr[rows, hc] * jnp.exp(gcb) - auw[hr, hd:]).astype(qp_scr.dtype)
            kd_t = (kn_scr[rows, hc] * jnp.exp(gcw_scr[last, hc] - gcb)).T
            bw = _dot(kd_t, uw[hr])
            sc = slice((h % ph) * hd, (h % ph + 1) * hd)
            bm_scr[c, h // ph, :, sc] = bw[:, :hd]
            w2_scr[c, h // ph, :, sc] = bw[:, hd:].astype(w2_scr.dtype)
    for c in range(nch):
        last = slice((c + 1) * CHUNK - 1, (c + 1) * CHUNK)
        ee_scr[c * CARRY_ROWS:(c + 1) * CARRY_ROWS, :] = jnp.broadcast_to(jnp.exp(gcw_scr[last, :]), (CARRY_ROWS, w))

    left = _mask01((hd, PACK), lambda r, c: c < hd)
    ng = ng_ref[...]

    states = [s_scr[p] for p in range(npair)]
    for c in range(nch):
        rows = slice(c * CHUNK, (c + 1) * CHUNK)
        ee = ee_scr[c * CARRY_ROWS:c * CARRY_ROWS + 1, :]
        prods = []
        for p in range(npair):
            pc = slice(p * PACK, (p + 1) * PACK)
            s = states[p]
            s_bd = jnp.concatenate([jnp.where(left, s, 0.0), jnp.where(left, 0.0, s)], axis=0).astype(BF16)
            lhs = jnp.concatenate([w2_scr[c, p], qp_scr[rows, pc]], axis=0)
            prods.append(jnp.dot(lhs, s_bd, preferred_element_type=F32))
        for p, r in enumerate(prods):
            pc = slice(p * PACK, (p + 1) * PACK)
            states[p] = ee[:, pc] * states[p] + bm_scr[c, p] - r[:hd]
            o = au_scr[rows, pc] + r[hd:]
            for side in range(ph):
                hc = slice((p * ph + side) * hd, (p * ph + side + 1) * hd)
                z = z_ref[rows, hc].astype(F32)
                o_ref[rows, hc] = (_rms(o[:, side * hd:(side + 1) * hd], ng) * (z * _sigmoid(z))).astype(o_ref.dtype)
    for p in range(npair):
        s_scr[p] = states[p]


def _gdn(proj, small, conv_w, alog, dtb, norm_g, *, bsz, seq, ts, nh, hd):
    t = bsz * seq
    nt = seq // ts
    w = nh * hd
    nch = ts // CHUNK
    assert PACK % CHUNK == 0 and PACK % hd == 0 and nh % (PACK // CHUNK) == 0 and nh % (PACK // hd) == 0
    tok = lambda cb: pl.BlockSpec((ts, w), lambda b, j: (b * nt + j, cb))
    full = lambda shape: pl.BlockSpec(shape, lambda b, j: (0, 0))
    return pl.pallas_call(
        functools.partial(_gdn_kernel, ts=ts, nh=nh, hd=hd),
        grid=(bsz, nt),
        in_specs=[tok(0), tok(1), tok(2), tok(3),
                  pl.BlockSpec((ts, LANES), lambda b, j: (b * nt + j, 0)),
                  full((CONV_TAPS, 3 * w)), full((1, LANES)), full((1, LANES)), full((1, hd))],
        out_specs=pl.BlockSpec((ts, w), lambda b, j: (b * nt + j, 0)),
        out_shape=jax.ShapeDtypeStruct((t, w), BF16),
        scratch_shapes=[
            pltpu.VMEM((CARRY_ROWS + ts, 3 * w), F32),
            pltpu.VMEM((ts, w), F32), pltpu.VMEM((ts, w), F32), pltpu.VMEM((ts, w), F32),
            pltpu.VMEM((ts, w), F32), pltpu.VMEM((ts, w), F32),
            pltpu.VMEM((nch, nh * hd // PACK, hd, PACK), BF16),
            pltpu.VMEM((nch, nh * hd // PACK, hd, PACK), F32),
            pltpu.VMEM((ts, w), BF16), pltpu.VMEM((ts, w), F32),
            pltpu.VMEM((nch * CARRY_ROWS, w), F32),
            pltpu.VMEM((nh * hd // PACK, hd, PACK), F32),
        ],
        compiler_params=_params(("parallel", "arbitrary")),
        name="gdn",
    )(proj, proj, proj, proj, small, conv_w, alog, dtb, norm_g)


def _gla_kernel(q_ref, k_ref, v_ref, r_ref, sm_ref, wg_ref, bg_ref, ng_ref, o_ref, st_scr, *, ts, nh, dk, dv):
    nch = ts // CHUNK
    gh = PACK // CHUNK
    ngrp = nh // gh

    @pl.when(pl.program_id(1) == 0)
    def _():
        st_scr[...] = jnp.zeros_like(st_scr)

    lg = _dot(sm_ref[...], wg_ref[...]) + bg_ref[...]
    log_fg = (jnp.minimum(lg, 0.0) - jnp.log(1.0 + jnp.exp(-jnp.abs(lg)))) / GLA_GATE_TAU
    bc_all = _dot_exact_lhs(_chunk_cumsum_matrix(ts), log_fg)

    chunk_p = (CHUNK, PACK)
    incl_p = _mask01(chunk_p, lambda r, c: r >= c % CHUNK)
    head_p = [_mask01(chunk_p, lambda r, c, hh=hh: c // CHUNK == hh) for hh in range(gh)]
    kbd01 = _bf01((PACK, gh * dk), lambda r, c: r // CHUNK == c // dk)
    ng = ng_ref[...]
    scale = dk ** -0.5

    def blockrows(mat_p):
        return jnp.concatenate([jnp.where(m, mat_p, 0.0) for m in head_p], axis=0).astype(BF16)

    qd, kd, ge, a_ps = [], [], [], []
    for c in range(nch):
        rows = slice(c * CHUNK, (c + 1) * CHUNK)
        bc = bc_all[rows]
        bref = bc[CHUNK // 2:CHUNK // 2 + 1, :]
        bend = bc[CHUNK - 1:CHUNK, :]
        q = q_ref[rows, :].astype(F32) * scale
        k = k_ref[rows, :].astype(F32)
        qe = (q * jnp.exp(bc - bref)).astype(BF16)
        ke = (k * jnp.exp(bref - bc)).astype(BF16)
        qd.append((q * jnp.exp(bc)).astype(BF16))
        kd.append((k * jnp.exp(bend - bc)).astype(BF16))
        ge.append(jnp.exp(bend))
        for gi in range(ngrp):
            gcols = slice(gi * gh * dk, (gi + 1) * gh * dk)
            kebd = jnp.concatenate([ke[:, gcols]] * gh, axis=0) * kbd01
            a = lax.dot_general(qe[:, gcols], kebd, (((1,), (1,)), ((), ())), preferred_element_type=F32)
            a_ps.append(jnp.where(incl_p, a, 0.0))

    oi = []
    for c in range(nch):
        rows = slice(c * CHUNK, (c + 1) * CHUNK)
        for gi in range(ngrp):
            v4 = jnp.concatenate([v_ref[rows, (gi * gh + hh) * dv:(gi * gh + hh + 1) * dv] for hh in range(gh)], axis=0)
            oi.append(jnp.dot(blockrows(a_ps[c * ngrp + gi]), v4, preferred_element_type=F32))
    kvt = []
    for c in range(nch):
        rows = slice(c * CHUNK, (c + 1) * CHUNK)
        for h in range(nh):
            v_t = v_ref[rows, h * dv:(h + 1) * dv].astype(F32).T
            kvt.append(_dot(v_t, kd[c][:, h * dk:(h + 1) * dk]))

    st = [st_scr[h] for h in range(nh)]
    for c in range(nch):
        rows = slice(c * CHUNK, (c + 1) * CHUNK)
        for h in range(nh):
            ks = slice(h * dk, (h + 1) * dk)
            vs = slice(h * dv, (h + 1) * dv)
            hr = slice((h % gh) * CHUNK, (h % gh + 1) * CHUNK)
            o = _dot_nt(qd[c][:, ks], st[h]) + oi[c * ngrp + h // gh][hr]
            st[h] = ge[c][:, ks] * st[h] + kvt[c * nh + h]
            r = r_ref[rows, vs].astype(F32)
            o_ref[rows, vs] = (_rms(o, ng) * (r * _sigmoid(r))).astype(o_ref.dtype)
    for h in range(nh):
        st_scr[h] = st[h]


def _gla(proj, small, w_gate, b_gate, norm_g, *, bsz, seq, ts, nh, dk, dv, col0):
    t = bsz * seq
    nt = seq // ts
    kw, vw = nh * dk, nh * dv
    assert nh % (PACK // CHUNK) == 0
    spec = lambda width, start: pl.BlockSpec((ts, width), lambda b, j: (b * nt + j, start // width))
    full = lambda shape: pl.BlockSpec(shape, lambda b, j: (0, 0))
    return pl.pallas_call(
        functools.partial(_gla_kernel, ts=ts, nh=nh, dk=dk, dv=dv),
        grid=(bsz, nt),
        in_specs=[spec(kw, col0), spec(kw, col0 + kw), spec(vw, col0 + 2 * kw), spec(vw, col0 + 2 * kw + vw),
                  pl.BlockSpec((ts, LANES), lambda b, j: (b * nt + j, 0)),
                  full((LANES, kw)), full((1, kw)), full((1, dv))],
        out_specs=pl.BlockSpec((ts, vw), lambda b, j: (b * nt + j, 0)),
        out_shape=jax.ShapeDtypeStruct((t, vw), BF16),
        scratch_shapes=[pltpu.VMEM((nh, dv, dk), F32)],
        compiler_params=_params(("parallel", "arbitrary")),
        name="gla",
    )(proj, proj, proj, proj, small, w_gate, b_gate, norm_g)


def _merge_kernel(x_ref, oa_ref, ob_ref, ga_ref, gb_ref, wa_ref, wb_ref, wo_ref, out_ref):
    ya = jnp.dot(oa_ref[...], wa_ref[...], preferred_element_type=F32)
    yb = jnp.dot(ob_ref[...], wb_ref[...], preferred_element_type=F32)
    merged = _sigmoid(ga_ref[...].astype(F32)) * ya + _sigmoid(gb_ref[...].astype(F32)) * yb
    out_ref[...] = x_ref[...] + _dot(merged, wo_ref[...])


def _merge(x2, o_a, o_b, proj, w_a, w_b, w_o, *, tm, gate_col):
    t, d = x2.shape
    tok = lambda cb: pl.BlockSpec((tm, d), lambda i: (i, cb))
    wsp = pl.BlockSpec((d, d), lambda i: (0, 0))
    return pl.pallas_call(
        _merge_kernel,
        grid=(t // tm,),
        in_specs=[tok(0), tok(0), tok(0), tok(gate_col // d), tok(gate_col // d + 1), wsp, wsp, wsp],
        out_specs=tok(0),
        out_shape=jax.ShapeDtypeStruct((t, d), F32),
        compiler_params=_params(("parallel",)),
        name="merge",
    )(x2, o_a, o_b, proj, proj, w_a, w_b, w_o)


def _memkv_kernel(m_ref, g_ref, wk_ref, wv_ref, k_ref, v_ref):
    m = _rms(m_ref[...], g_ref[...]).astype(BF16)
    k_ref[...] = jnp.dot(m, wk_ref[...], preferred_element_type=F32).astype(BF16)
    v_ref[...] = jnp.dot(m, wv_ref[...], preferred_element_type=F32).astype(BF16)


def _memkv(mem2, g, wk, wv, *, n_mem):
    t, d = mem2.shape
    tok = pl.BlockSpec((n_mem, d), lambda b: (b, 0))
    wsp = pl.BlockSpec((d, d), lambda b: (0, 0))
    return pl.pallas_call(
        _memkv_kernel,
        grid=(t // n_mem,),
        in_specs=[tok, pl.BlockSpec((1, d), lambda b: (0, 0)), wsp, wsp],
        out_specs=[tok, tok],
        out_shape=[jax.ShapeDtypeStruct((t, d), BF16)] * 2,
        compiler_params=_params(("parallel",)),
        name="memkv",
    )(mem2, g, wk, wv)


def _xattn_kernel(x_ref, g_ref, k_ref, v_ref, wq_ref, wo_ref, out_ref, *, nh):
    x = x_ref[...]
    d = x.shape[-1]
    hd = d // nh
    q = _dot(_rms(x, g_ref[...]), wq_ref[...])
    cols = [slice(h * hd, (h + 1) * hd) for h in range(nh)]
    qb = q.astype(BF16)
    scores = [_dot_nt(qb[:, cs], k_ref[:, cs]) * (hd ** -0.5) for cs in cols]
    probs = []
    for s in scores:
        e = jnp.exp(s - jnp.max(s, axis=-1, keepdims=True))
        probs.append((e / jnp.sum(e, axis=-1, keepdims=True)).astype(BF16))
    o = jnp.concatenate([jnp.dot(p, v_ref[:, cs], preferred_element_type=F32) for p, cs in zip(probs, cols)], axis=1)
    out_ref[...] = x + _dot(o, wo_ref[...])


def _xattn(x2, g, k_mem, v_mem, wq, wo, *, bsz, seq, tm, n_mem):
    t, d = x2.shape
    nt = seq // tm
    tok = pl.BlockSpec((tm, d), lambda b, j: (b * nt + j, 0))
    mem = pl.BlockSpec((n_mem, d), lambda b, j: (b, 0))
    wsp = pl.BlockSpec((d, d), lambda b, j: (0, 0))
    return pl.pallas_call(
        functools.partial(_xattn_kernel, nh=XATTN_HEADS),
        grid=(bsz, nt),
        in_specs=[tok, pl.BlockSpec((1, d), lambda b, j: (0, 0)), mem, mem, wsp, wsp],
        out_specs=tok,
        out_shape=jax.ShapeDtypeStruct((t, d), F32),
        compiler_params=_params(("parallel", "parallel")),
        name="xattn",
    )(x2, g, k_mem, v_mem, wq, wo)


def _mlp_kernel(x_ref, g_ref, w1_ref, w2_ref, gf_ref, out_ref, h_scr, acc_scr, *, final_norm):
    j = pl.program_id(1)

    @pl.when(j == 0)
    def _():
        h_scr[...] = _rms(x_ref[...], g_ref[...]).astype(BF16)
        acc_scr[...] = x_ref[...]

    a = jnp.dot(h_scr[...], w1_ref[...], preferred_element_type=F32)
    acc_scr[...] += _dot(jnp.square(jnp.maximum(a, 0.0)), w2_ref[...])

    @pl.when(j == pl.num_programs(1) - 1)
    def _():
        y = acc_scr[...]
        out_ref[...] = _rms(y, gf_ref[...]) if final_norm else y


def _mlp(x2, g, w1, w2, g_final, *, tm, tf, final_norm):
    t, d = x2.shape
    ff = w1.shape[1]
    tok = pl.BlockSpec((tm, d), lambda i, j: (i, 0))
    vec = pl.BlockSpec((1, d), lambda i, j: (0, 0))
    return pl.pallas_call(
        functools.partial(_mlp_kernel, final_norm=final_norm),
        grid=(t // tm, ff // tf),
        in_specs=[tok, vec, pl.BlockSpec((d, tf), lambda i, j: (0, j)),
                  pl.BlockSpec((tf, d), lambda i, j: (j, 0)), vec],
        out_specs=tok,
        out_shape=jax.ShapeDtypeStruct((t, d), F32),
        scratch_shapes=[pltpu.VMEM((tm, d), BF16), pltpu.VMEM((tm, d), F32)],
        compiler_params=_params(("parallel", "arbitrary")),
        name="mlp",
    )(x2, g, w1, w2, g_final)


def _pad_lanes(v, offset):
    return jnp.zeros((1, LANES), F32).at[0, offset:offset + v.shape[0]].set(v.astype(F32))


def _tile(n, pref):
    while n % pref:
        pref //= 2
    return pref


def kernel(x, mem, norm_mix_g, w_in, gdn_conv_w, gdn_a_log, gdn_dt_bias, gdn_norm_g, gla_w_gate2, gla_b_gate, gla_norm_g, w_branch_gdn, w_branch_gla, w_out, norm_xattn_g, norm_mem_g, xattn_wq, xattn_wk, xattn_wv, xattn_wo, norm_mlp_g, mlp_w1, mlp_w2, norm_final_g):
    bsz, seq, d = x.shape
    n_mem = mem.shape[1]
    depth = w_in.shape[0]
    gdn_heads = gdn_a_log.shape[1]
    gdn_hd = gdn_norm_g.shape[1]
    gdn_w = gdn_heads * gdn_hd
    gla_kw = gla_b_gate.shape[1]
    gla_dv = gla_norm_g.shape[1]
    gla_vw = w_branch_gla.shape[1]
    gla_heads = gla_vw // gla_dv
    gla_dk = gla_kw // gla_heads
    rank = gla_w_gate2.shape[1]
    assert depth >= 1 and 2 * gdn_heads + rank <= LANES and seq % CHUNK == 0

    sizes = (gdn_w, gdn_w, gdn_w, gdn_w, gdn_heads, gdn_heads, gla_kw, gla_kw, gla_vw, gla_vw, rank, d, d)
    offs = [0]
    for s in sizes:
        offs.append(offs[-1] + s)
    assert offs[-1] == w_in.shape[2]
    big_cols = [0, 1, 2, 3, 6, 7, 8, 9, 11, 12]
    gla_col0 = 4 * gdn_w
    gate_col = gla_col0 + 2 * gla_kw + 2 * gla_vw

    t = bsz * seq
    x2 = x.reshape(t, d).astype(F32)
    mem2 = mem.reshape(bsz * n_mem, d).astype(F32)
    row = lambda v: v.reshape(1, -1).astype(F32)
    tm = _tile(t, 1024)
    ts = _tile(seq, 256)

    for i in range(depth):
        wi = w_in[i]
        w_big = jnp.concatenate([wi[:, offs[c]:offs[c + 1]] for c in big_cols], axis=1).astype(BF16)
        w_small = jnp.zeros((d, LANES), F32)
        w_small = w_small.at[:, 0:gdn_heads].set(wi[:, offs[4]:offs[5]])
        w_small = w_small.at[:, gdn_heads:2 * gdn_heads].set(wi[:, offs[5]:offs[6]])
        w_small = w_small.at[:, 2 * gdn_heads:2 * gdn_heads + rank].set(wi[:, offs[10]:offs[11]])
        w_gate = jnp.zeros((LANES, gla_kw), F32).at[2 * gdn_heads:2 * gdn_heads + rank].set(gla_w_gate2[i])

        proj, small = _inproj(x2, row(norm_mix_g[i]), w_big, w_small.astype(BF16), tm=tm, tn=_tile(w_big.shape[1], 2304))
        o_a = _gdn(proj, small, gdn_conv_w[i].astype(F32), _pad_lanes(gdn_a_log[i], 0),
                   _pad_lanes(gdn_dt_bias[i], 0), row(gdn_norm_g[i]),
                   bsz=bsz, seq=seq, ts=ts, nh=gdn_heads, hd=gdn_hd)
        o_b = _gla(proj, small, w_gate.astype(BF16), row(gla_b_gate[i]), row(gla_norm_g[i]),
                   bsz=bsz, seq=seq, ts=ts, nh=gla_heads, dk=gla_dk, dv=gla_dv, col0=gla_col0)
        x2 = _merge(x2, o_a, o_b, proj, w_branch_gdn[i].astype(BF16), w_branch_gla[i].astype(BF16),
                    w_out[i].astype(BF16), tm=_tile(t, 512), gate_col=gate_col)

        k_mem, v_mem = _memkv(mem2, row(norm_mem_g[i]), xattn_wk[i].astype(BF16), xattn_wv[i].astype(BF16), n_mem=n_mem)
        x2 = _xattn(x2, row(norm_xattn_g[i]), k_mem, v_mem, xattn_wq[i].astype(BF16), xattn_wo[i].astype(BF16),
                    bsz=bsz, seq=seq, tm=_tile(seq, 512), n_mem=n_mem)

        last = i == depth - 1
        x2 = _mlp(x2, row(norm_mlp_g[i]), mlp_w1[i].astype(BF16), mlp_w2[i].astype(BF16), row(norm_final_g),
                  tm=tm, tf=_tile(mlp_w1.shape[2], 1024), final_norm=last)
    return x2.reshape(bsz, seq, d).astype(x.dtype)
```

```python
import functools

import jax
import jax.numpy as jnp
from jax import lax
from jax.experimental import pallas as pl
from jax.experimental.pallas import tpu as pltpu

F32 = jnp.float32
BF16 = jnp.bfloat16

CHUNK = 64
CONV_TAPS = 4
GLA_GATE_TAU = 16.0
XATTN_HEADS = 4
NORM_EPS = 1e-6
NEG_LOG2E = -1.4426950408889634
LANES = 128
CARRY_ROWS = 8
INV_LEAF = 16
PACK = 256

VMEM_LIMIT = 48 * 1024 * 1024


def _params(semantics):
    return pltpu.CompilerParams(dimension_semantics=semantics, vmem_limit_bytes=VMEM_LIMIT)


def _dot(a, b):
    return jnp.dot(a.astype(BF16), b.astype(BF16), preferred_element_type=F32)


def _dot_nt(a, b):
    return lax.dot_general(a.astype(BF16), b.astype(BF16), (((1,), (1,)), ((), ())),
                           preferred_element_type=F32)


def _split(a):
    hi = a.astype(BF16)
    lo = (a - hi.astype(F32)).astype(BF16)
    return hi, lo


def _dot_exact_lhs(tri_bf16, b):
    bh, bl = _split(b)
    d = functools.partial(jnp.dot, preferred_element_type=F32)
    return d(tri_bf16, bh) + d(tri_bf16, bl)


def _sigmoid(x):
    return 1.0 / (1.0 + jnp.exp2(x * NEG_LOG2E))


def _softplus(x):
    return jnp.maximum(x, 0.0) + jnp.log(1.0 + jnp.exp(-jnp.abs(x)))


def _rms(x, g):
    return x * lax.rsqrt(jnp.mean(x * x, axis=-1, keepdims=True) + NORM_EPS) * g


def _chunk_cumsum_matrix(ts):
    r = lax.broadcasted_iota(jnp.int32, (ts, ts), 0)
    c = lax.broadcasted_iota(jnp.int32, (ts, ts), 1)
    return jnp.where((r >= c) & ((r // CHUNK) == (c // CHUNK)), 1.0, 0.0).astype(BF16)


def _inproj_kernel(x_ref, g_ref, wbig_ref, wsm_ref, big_ref, sm_ref, h_scr):
    @pl.when(pl.program_id(1) == 0)
    def _():
        h = _rms(x_ref[...], g_ref[...]).astype(BF16)
        h_scr[...] = h
        sm_ref[...] = jnp.dot(h, wsm_ref[...], preferred_element_type=F32)

    big_ref[...] = jnp.dot(h_scr[...], wbig_ref[...], preferred_element_type=F32).astype(big_ref.dtype)


def _inproj(x2, g, w_big, w_small, *, tm, tn):
    t, d = x2.shape
    n = w_big.shape[1]
    return pl.pallas_call(
        _inproj_kernel,
        grid=(t // tm, n // tn),
        in_specs=[
            pl.BlockSpec((tm, d), lambda i, j: (i, 0)),
            pl.BlockSpec((1, d), lambda i, j: (0, 0)),
            pl.BlockSpec((d, tn), lambda i, j: (0, j)),
            pl.BlockSpec((d, LANES), lambda i, j: (0, 0)),
        ],
        out_specs=[
            pl.BlockSpec((tm, tn), lambda i, j: (i, j)),
            pl.BlockSpec((tm, LANES), lambda i, j: (i, 0)),
        ],
        out_shape=[
            jax.ShapeDtypeStruct((t, n), BF16),
            jax.ShapeDtypeStruct((t, LANES), F32),
        ],
        scratch_shapes=[pltpu.VMEM((tm, d), BF16)],
        compiler_params=_params(("parallel", "arbitrary")),
        name="inproj",
    )(x2, g, w_big, w_small)


def _mask01(shape, fn):
    r = lax.broadcasted_iota(jnp.int32, shape, 0)
    c = lax.broadcasted_iota(jnp.int32, shape, 1)
    return fn(r, c)


def _bf01(shape, fn):
    return jnp.where(_mask01(shape, fn), 1.0, 0.0).astype(BF16)


def _dot_exact_rhs(a, e_bf16):
    ah, al = _split(a)
    d = functools.partial(jnp.dot, preferred_element_type=F32)
    return d(ah, e_bf16) + d(al, e_bf16)


def _mm_packed(lhs, bp, bd01):
    reps = PACK // CHUNK
    bd = jnp.concatenate([bp.astype(BF16)] * reps, axis=0) * bd01
    return jnp.dot(lhs.astype(BF16), bd, preferred_element_type=F32)


def _packed_unit_lower_inverse(lows, eye_p, leaf_p, bd01):
    mm = functools.partial(_mm_packed, bd01=bd01)
    stack = lambda a, b: jnp.concatenate([a, b], axis=0)
    dg = [jnp.where(leaf_p, l, 0.0) for l in lows]
    off = [l - d for l, d in zip(lows, dg)]
    d2 = [mm(d, d) for d in dg]
    p = [eye_p - d for d in dg]
    r = [mm(stack(pi, di), di) for pi, di in zip(p, d2)]
    p = [pi + ri[:CHUNK] for pi, ri in zip(p, r)]
    d4 = [ri[CHUNK:] for ri in r]
    r = [mm(stack(pi, di), di) for pi, di in zip(p, d4)]
    p = [pi + ri[:CHUNK] for pi, ri in zip(p, r)]
    d8 = [ri[CHUNK:] for ri in r]
    p = [pi + mm(pi, di) for pi, di in zip(p, d8)]
    n = [mm(pi, oi) for pi, oi in zip(p, off)]
    n2 = [mm(ni, ni) for ni in n]
    rr = [eye_p - ni for ni in n]
    rr = [ri + mm(ri, ni) for ri, ni in zip(rr, n2)]
    return [mm(ri, pi) for ri, pi in zip(rr, p)]


def _gdn_kernel(q_ref, k_ref, v_ref, z_ref, sm_ref, cw_ref, alog_ref, dtb_ref, ng_ref, o_ref,
                xbuf, qn_scr, kn_scr, vv_scr, gcw_scr, btw_scr, w2_scr, bm_scr, qp_scr, au_scr, ee_scr,
                s_scr, *, ts, nh, hd):
    w = nh * hd
    nch = ts // CHUNK
    gh = PACK // CHUNK
    ngrp = nh // gh
    ph = PACK // hd
    npair = nh // ph

    @pl.when(pl.program_id(1) == 0)
    def _():
        xbuf[...] = jnp.zeros_like(xbuf)
        s_scr[...] = jnp.zeros_like(s_scr)

    sub = lax.broadcasted_iota(jnp.int32, (CARRY_ROWS, hd), 0)
    for part, (src, dst) in enumerate(((q_ref, qn_scr), (k_ref, kn_scr), (v_ref, vv_scr))):
        for h in range(nh):
            cs = slice(part * w + h * hd, part * w + (h + 1) * hd)
            x = src[:, h * hd:(h + 1) * hd].astype(F32)
            prev = xbuf[:, cs]
            y = cw_ref[CONV_TAPS - 1:CONV_TAPS, cs] * x
            for sft in range(1, CONV_TAPS):
                r = pltpu.roll(x, sft, 0)
                head_rows = jnp.where(sub < sft, pltpu.roll(prev, sft, 0), r[:CARRY_ROWS])
                r = jnp.concatenate([head_rows, r[CARRY_ROWS:]], axis=0)
                y = y + cw_ref[CONV_TAPS - 1 - sft:CONV_TAPS - sft, cs] * r
            xbuf[:, cs] = x[ts - CARRY_ROWS:]
            y = y * _sigmoid(y)
            if part < 2:
                y = y * (lax.rsqrt(jnp.sum(y * y, axis=-1, keepdims=True) + NORM_EPS)
                         * (hd ** -0.5 if part == 0 else 1.0))
            dst[:, h * hd:(h + 1) * hd] = y

    sm = sm_ref[...]
    g = -jnp.exp(alog_ref[...]) * _softplus(sm + dtb_ref[...])
    gc = _dot_exact_lhs(_chunk_cumsum_matrix(ts), g)
    beta = _sigmoid(sm)
    gcw_scr[...] = _dot_exact_rhs(gc, _bf01((LANES, w), lambda r, c: r == c // hd))
    btw_scr[...] = _dot_exact_rhs(beta, _bf01((LANES, w), lambda r, c: r == nh + c // hd))

    tile_p = (ts, PACK)
    incl_p = _mask01(tile_p, lambda r, c: r % CHUNK >= c % CHUNK)
    diag_p = _mask01(tile_p, lambda r, c: r % CHUNK == c % CHUNK)
    same_chunk = _bf01((ts, ts), lambda r, c: r // CHUNK == c // CHUNK)
    chunk_p = (CHUNK, PACK)
    strict_p = _mask01(chunk_p, lambda r, c: r > c % CHUNK)
    leaf_p = _mask01(chunk_p, lambda r, c: r // INV_LEAF == (c % CHUNK) // INV_LEAF)
    eye_p = jnp.where(_mask01(chunk_p, lambda r, c: r == c % CHUNK), 1.0, 0.0).astype(F32)
    head_p = [_mask01(chunk_p, lambda r, c, hh=hh: c // CHUNK == hh) for hh in range(gh)]
    bd01 = _bf01((PACK, PACK), lambda r, c: r // CHUNK == c // CHUNK)
    kbd01 = _bf01((PACK, gh * hd), lambda r, c: r // CHUNK == c // hd)

    dec_g, bcol_g = [], []
    for gi in range(ngrp):
        gcol = _dot_exact_rhs(gc, _bf01((LANES, PACK), lambda r, c, gi=gi: r == gi * gh + c // CHUNK))
        grow = _dot_exact_lhs(same_chunk, jnp.where(diag_p, gcol, 0.0))
        dec_g.append(jnp.exp(jnp.where(incl_p, gcol - grow, -jnp.inf)))
        bcol_g.append(_dot_exact_rhs(beta, _bf01((LANES, PACK), lambda r, c, gi=gi: r == nh + gi * gh + c // CHUNK)))

    chains = [(c, gi) for c in range(nch) for gi in range(ngrp)]
    lows, a_ps = [], []
    for c, gi in chains:
        rows = slice(c * CHUNK, (c + 1) * CHUNK)
        gcols = slice(gi * gh * hd, (gi + 1) * gh * hd)
        qg = qn_scr[rows, gcols].astype(BF16)
        kg = kn_scr[rows, gcols].astype(BF16)
        kbd = jnp.concatenate([kg] * gh, axis=0) * kbd01
        qk_kk = lax.dot_general(jnp.concatenate([qg, kg], axis=0), kbd, (((1,), (1,)), ((), ())),
                                preferred_element_type=F32)
        dec = dec_g[gi][rows]
        lows.append(jnp.where(strict_p, bcol_g[gi][rows] * qk_kk[CHUNK:] * dec, 0.0))
        a_ps.append(qk_kk[:CHUNK] * dec)

    tinvs = _packed_unit_lower_inverse(lows, eye_p, leaf_p, bd01)

    def blockrows(mat_p):
        return jnp.concatenate([jnp.where(m, mat_p, 0.0) for m in head_p], axis=0).astype(BF16)

    uws = []
    for (c, gi), tinv in zip(chains, tinvs):
        rows = slice(c * CHUNK, (c + 1) * CHUNK)
        rhs = []
        for hh in range(gh):
            hc = slice((gi * gh + hh) * hd, (gi * gh + hh + 1) * hd)
            bt = btw_scr[rows, hc]
            rhs.append(jnp.concatenate([bt * vv_scr[rows, hc],
                                        bt * jnp.exp(gcw_scr[rows, hc]) * kn_scr[rows, hc]], axis=1))
        uws.append(_dot(blockrows(tinv), jnp.concatenate(rhs, axis=0)))
    auws = [_dot(blockrows(a_p), uw) for a_p, uw in zip(a_ps, uws)]

    for (c, gi), uw, auw in zip(chains, uws, auws):
        rows = slice(c * CHUNK, (c + 1) * CHUNK)
        last = slice((c + 1) * CHUNK - 1, (c + 1) * CHUNK)
        for hh in range(gh):
            h = gi * gh + hh
            hc = slice(h * hd, (h + 1) * hd)
            hr = slice(hh * CHUNK, (hh + 1) * CHUNK)
            gcb = gcw_scr[rows, hc]
            au_scr[rows, hc] = auw[hr, :hd]
            qp_scr[rows, hc] = (qn_scr[rows, hc] * jnp.exp(gcb) - auw[hr, hd:]).astype(qp_scr.dtype)
            kd_t = (kn_scr[rows, hc] * jnp.exp(gcw_scr[last, hc] - gcb)).T
            bw = _dot(kd_t, uw[hr])
            sc = slice((h % ph) * hd, (h % ph + 1) * hd)
            bm_scr[c, h // ph, :, sc] = bw[:, :hd]
            w2_scr[c, h // ph, :, sc] = bw[:, hd:].astype(w2_scr.dtype)
    for c in range(nch):
        last = slice((c + 1) * CHUNK - 1, (c + 1) * CHUNK)
        ee_scr[c * CARRY_ROWS:(c + 1) * CARRY_ROWS, :] = jnp.broadcast_to(jnp.exp(gcw_scr[last, :]), (CARRY_ROWS, w))

    left = _mask01((hd, PACK), lambda r, c: c < hd)
    ng = ng_ref[...]

    states = [s_scr[p] for p in range(npair)]
    for c in range(nch):
        rows = slice(c * CHUNK, (c + 1) * CHUNK)
        ee = ee_scr[c * CARRY_ROWS:c * CARRY_ROWS + 1, :]
        prods = []
        for p in range(npair):
            pc = slice(p * PACK, (p + 1) * PACK)
            s = states[p]
            s_bd = jnp.concatenate([jnp.where(left, s, 0.0), jnp.where(left, 0.0, s)], axis=0).astype(BF16)
            lhs = jnp.concatenate([w2_scr[c, p], qp_scr[rows, pc]], axis=0)
            prods.append(jnp.dot(lhs, s_bd, preferred_element_type=F32))
        for p, r in enumerate(prods):
            pc = slice(p * PACK, (p + 1) * PACK)
            states[p] = ee[:, pc] * states[p] + bm_scr[c, p] - r[:hd]
            o = au_scr[rows, pc] + r[hd:]
            for side in range(ph):
                hc = slice((p * ph + side) * hd, (p * ph + side + 1) * hd)
                z = z_ref[rows, hc].astype(F32)
                o_ref[rows, hc] = (_rms(o[:, side * hd:(side + 1) * hd], ng) * (z * _sigmoid(z))).astype(o_ref.dtype)
    for p in range(npair):
        s_scr[p] = states[p]


def _gdn(proj, small, conv_w, alog, dtb, norm_g, *, bsz, seq, ts, nh, hd):
    t = bsz * seq
    nt = seq // ts
    w = nh * hd
    nch = ts // CHUNK
    assert PACK % CHUNK == 0 and PACK % hd == 0 and nh % (PACK // CHUNK) == 0 and nh % (PACK // hd) == 0
    tok = lambda cb: pl.BlockSpec((ts, w), lambda b, j: (b * nt + j, cb))
    full = lambda shape: pl.BlockSpec(shape, lambda b, j: (0, 0))
    return pl.pallas_call(
        functools.partial(_gdn_kernel, ts=ts, nh=nh, hd=hd),
        grid=(bsz, nt),
        in_specs=[tok(0), tok(1), tok(2), tok(3),
                  pl.BlockSpec((ts, LANES), lambda b, j: (b * nt + j, 0)),
                  full((CONV_TAPS, 3 * w)), full((1, LANES)), full((1, LANES)), full((1, hd))],
        out_specs=pl.BlockSpec((ts, w), lambda b, j: (b * nt + j, 0)),
        out_shape=jax.ShapeDtypeStruct((t, w), BF16),
        scratch_shapes=[
            pltpu.VMEM((CARRY_ROWS, 3 * w), F32),
            pltpu.VMEM((ts, w), F32), pltpu.VMEM((ts, w), F32), pltpu.VMEM((ts, w), F32),
            pltpu.VMEM((ts, w), F32), pltpu.VMEM((ts, w), F32),
            pltpu.VMEM((nch, nh * hd // PACK, hd, PACK), BF16),
            pltpu.VMEM((nch, nh * hd // PACK, hd, PACK), F32),
            pltpu.VMEM((ts, w), BF16), pltpu.VMEM((ts, w), F32),
            pltpu.VMEM((nch * CARRY_ROWS, w), F32),
            pltpu.VMEM((nh * hd // PACK, hd, PACK), F32),
        ],
        compiler_params=_params(("parallel", "arbitrary")),
        name="gdn",
    )(proj, proj, proj, proj, small, conv_w, alog, dtb, norm_g)


def _gla_kernel(q_ref, k_ref, v_ref, r_ref, sm_ref, wg_ref, bg_ref, ng_ref, o_ref, st_scr, *, ts, nh, dk, dv):
    nch = ts // CHUNK
    gh = PACK // CHUNK
    ngrp = nh // gh

    @pl.when(pl.program_id(1) == 0)
    def _():
        st_scr[...] = jnp.zeros_like(st_scr)

    lg = _dot(sm_ref[...], wg_ref[...]) + bg_ref[...]
    log_fg = (jnp.minimum(lg, 0.0) - jnp.log(1.0 + jnp.exp(-jnp.abs(lg)))) / GLA_GATE_TAU
    bc_all = _dot_exact_lhs(_chunk_cumsum_matrix(ts), log_fg)

    chunk_p = (CHUNK, PACK)
    incl_p = _mask01(chunk_p, lambda r, c: r >= c % CHUNK)
    head_p = [_mask01(chunk_p, lambda r, c, hh=hh: c // CHUNK == hh) for hh in range(gh)]
    kbd01 = _bf01((PACK, gh * dk), lambda r, c: r // CHUNK == c // dk)
    ng = ng_ref[...]
    scale = dk ** -0.5

    def blockrows(mat_p):
        return jnp.concatenate([jnp.where(m, mat_p, 0.0) for m in head_p], axis=0).astype(BF16)

    qd, kd, ge, a_ps = [], [], [], []
    for c in range(nch):
        rows = slice(c * CHUNK, (c + 1) * CHUNK)
        bc = bc_all[rows]
        bref = bc[CHUNK // 2:CHUNK // 2 + 1, :]
        bend = bc[CHUNK - 1:CHUNK, :]
        q = q_ref[rows, :].astype(F32) * scale
        k = k_ref[rows, :].astype(F32)
        qe = (q * jnp.exp(bc - bref)).astype(BF16)
        ke = (k * jnp.exp(bref - bc)).astype(BF16)
        qd.append((q * jnp.exp(bc)).astype(BF16))
        kd.append((k * jnp.exp(bend - bc)).astype(BF16))
        ge.append(jnp.exp(bend))
        for gi in range(ngrp):
            gcols = slice(gi * gh * dk, (gi + 1) * gh * dk)
            kebd = jnp.concatenate([ke[:, gcols]] * gh, axis=0) * kbd01
            a = lax.dot_general(qe[:, gcols], kebd, (((1,), (1,)), ((), ())), preferred_element_type=F32)
            a_ps.append(jnp.where(incl_p, a, 0.0))

    oi = []
    for c in range(nch):
        rows = slice(c * CHUNK, (c + 1) * CHUNK)
        for gi in range(ngrp):
            v4 = jnp.concatenate([v_ref[rows, (gi * gh + hh) * dv:(gi * gh + hh + 1) * dv] for hh in range(gh)], axis=0)
            oi.append(jnp.dot(blockrows(a_ps[c * ngrp + gi]), v4, preferred_element_type=F32))
    kvt = []
    for c in range(nch):
        rows = slice(c * CHUNK, (c + 1) * CHUNK)
        for h in range(nh):
            v_t = v_ref[rows, h * dv:(h + 1) * dv].astype(F32).T
            kvt.append(_dot(v_t, kd[c][:, h * dk:(h + 1) * dk]))

    st = [st_scr[h] for h in range(nh)]
    for c in range(nch):
        rows = slice(c * CHUNK, (c + 1) * CHUNK)
        for h in range(nh):
            ks = slice(h * dk, (h + 1) * dk)
            vs = slice(h * dv, (h + 1) * dv)
            hr = slice((h % gh) * CHUNK, (h % gh + 1) * CHUNK)
            o = _dot_nt(qd[c][:, ks], st[h]) + oi[c * ngrp + h // gh][hr]
            st[h] = ge[c][:, ks] * st[h] + kvt[c * nh + h]
            r = r_ref[rows, vs].astype(F32)
            o_ref[rows, vs] = (_rms(o, ng) * (r * _sigmoid(r))).astype(o_ref.dtype)
    for h in range(nh):
        st_scr[h] = st[h]


def _gla(proj, small, w_gate, b_gate, norm_g, *, bsz, seq, ts, nh, dk, dv, col0):
    t = bsz * seq
    nt = seq // ts
    kw, vw = nh * dk, nh * dv
    assert nh % (PACK // CHUNK) == 0
    spec = lambda width, start: pl.BlockSpec((ts, width), lambda b, j: (b * nt + j, start // width))
    full = lambda shape: pl.BlockSpec(shape, lambda b, j: (0, 0))
    return pl.pallas_call(
        functools.partial(_gla_kernel, ts=ts, nh=nh, dk=dk, dv=dv),
        grid=(bsz, nt),
        in_specs=[spec(kw, col0), spec(kw, col0 + kw), spec(vw, col0 + 2 * kw), spec(vw, col0 + 2 * kw + vw),
                  pl.BlockSpec((ts, LANES), lambda b, j: (b * nt + j, 0)),
                  full((LANES, kw)), full((1, kw)), full((1, dv))],
        out_specs=pl.BlockSpec((ts, vw), lambda b, j: (b * nt + j, 0)),
        out_shape=jax.ShapeDtypeStruct((t, vw), BF16),
        scratch_shapes=[pltpu.VMEM((nh, dv, dk), F32)],
        compiler_params=_params(("parallel", "arbitrary")),
        name="gla",
    )(proj, proj, proj, proj, small, w_gate, b_gate, norm_g)


def _merge_kernel(x_ref, oa_ref, ob_ref, ga_ref, gb_ref, wa_ref, wb_ref, wo_ref, out_ref):
    ya = jnp.dot(oa_ref[...], wa_ref[...], preferred_element_type=F32)
    yb = jnp.dot(ob_ref[...], wb_ref[...], preferred_element_type=F32)
    merged = _sigmoid(ga_ref[...].astype(F32)) * ya + _sigmoid(gb_ref[...].astype(F32)) * yb
    out_ref[...] = x_ref[...] + _dot(merged, wo_ref[...])


def _merge(x2, o_a, o_b, proj, w_a, w_b, w_o, *, tm, gate_col):
    t, d = x2.shape
    tok = lambda cb: pl.BlockSpec((tm, d), lambda i: (i, cb))
    wsp = pl.BlockSpec((d, d), lambda i: (0, 0))
    return pl.pallas_call(
        _merge_kernel,
        grid=(t // tm,),
        in_specs=[tok(0), tok(0), tok(0), tok(gate_col // d), tok(gate_col // d + 1), wsp, wsp, wsp],
        out_specs=tok(0),
        out_shape=jax.ShapeDtypeStruct((t, d), F32),
        compiler_params=_params(("parallel",)),
        name="merge",
    )(x2, o_a, o_b, proj, proj, w_a, w_b, w_o)


def _memkv_kernel(m_ref, g_ref, wk_ref, wv_ref, k_ref, v_ref):
    m = _rms(m_ref[...], g_ref[...]).astype(BF16)
    k_ref[...] = jnp.dot(m, wk_ref[...], preferred_element_type=F32).astype(BF16)
    v_ref[...] = jnp.dot(m, wv_ref[...], preferred_element_type=F32).astype(BF16)


def _memkv(mem2, g, wk, wv, *, n_mem):
    t, d = mem2.shape
    tok = pl.BlockSpec((n_mem, d), lambda b: (b, 0))
    wsp = pl.BlockSpec((d, d), lambda b: (0, 0))
    return pl.pallas_call(
        _memkv_kernel,
        grid=(t // n_mem,),
        in_specs=[tok, pl.BlockSpec((1, d), lambda b: (0, 0)), wsp, wsp],
        out_specs=[tok, tok],
        out_shape=[jax.ShapeDtypeStruct((t, d), BF16)] * 2,
        compiler_params=_params(("parallel",)),
        name="memkv",
    )(mem2, g, wk, wv)


def _xattn_kernel(x_ref, g_ref, k_ref, v_ref, wq_ref, wo_ref, out_ref, *, nh):
    x = x_ref[...]
    d = x.shape[-1]
    hd = d // nh
    q = _dot(_rms(x, g_ref[...]), wq_ref[...])
    cols = [slice(h * hd, (h + 1) * hd) for h in range(nh)]
    qb = q.astype(BF16)
    scores = [_dot_nt(qb[:, cs], k_ref[:, cs]) * (hd ** -0.5) for cs in cols]
    probs = []
    for s in scores:
        e = jnp.exp(s - jnp.max(s, axis=-1, keepdims=True))
        probs.append((e / jnp.sum(e, axis=-1, keepdims=True)).astype(BF16))
    o = jnp.concatenate([jnp.dot(p, v_ref[:, cs], preferred_element_type=F32) for p, cs in zip(probs, cols)], axis=1)
    out_ref[...] = x + _dot(o, wo_ref[...])


def _xattn(x2, g, k_mem, v_mem, wq, wo, *, bsz, seq, tm, n_mem):
    t, d = x2.shape
    nt = seq // tm
    tok = pl.BlockSpec((tm, d), lambda b, j: (b * nt + j, 0))
    mem = pl.BlockSpec((n_mem, d), lambda b, j: (b, 0))
    wsp = pl.BlockSpec((d, d), lambda b, j: (0, 0))
    return pl.pallas_call(
        functools.partial(_xattn_kernel, nh=XATTN_HEADS),
        grid=(bsz, nt),
        in_specs=[tok, pl.BlockSpec((1, d), lambda b, j: (0, 0)), mem, mem, wsp, wsp],
        out_specs=tok,
        out_shape=jax.ShapeDtypeStruct((t, d), F32),
        compiler_params=_params(("parallel", "parallel")),
        name="xattn",
    )(x2, g, k_mem, v_mem, wq, wo)


def _mlp_kernel(x_ref, g_ref, w1_ref, w2_ref, gf_ref, out_ref, h_scr, acc_scr, *, final_norm):
    j = pl.program_id(1)

    @pl.when(j == 0)
    def _():
        h_scr[...] = _rms(x_ref[...], g_ref[...]).astype(BF16)
        acc_scr[...] = x_ref[...]

    a = jnp.dot(h_scr[...], w1_ref[...], preferred_element_type=F32)
    acc_scr[...] += _dot(jnp.square(jnp.maximum(a, 0.0)), w2_ref[...])

    @pl.when(j == pl.num_programs(1) - 1)
    def _():
        y = acc_scr[...]
        out_ref[...] = _rms(y, gf_ref[...]) if final_norm else y


def _mlp(x2, g, w1, w2, g_final, *, tm, tf, final_norm):
    t, d = x2.shape
    ff = w1.shape[1]
    tok = pl.BlockSpec((tm, d), lambda i, j: (i, 0))
    vec = pl.BlockSpec((1, d), lambda i, j: (0, 0))
    return pl.pallas_call(
        functools.partial(_mlp_kernel, final_norm=final_norm),
        grid=(t // tm, ff // tf),
        in_specs=[tok, vec, pl.BlockSpec((d, tf), lambda i, j: (0, j)),
                  pl.BlockSpec((tf, d), lambda i, j: (j, 0)), vec],
        out_specs=tok,
        out_shape=jax.ShapeDtypeStruct((t, d), F32),
        scratch_shapes=[pltpu.VMEM((tm, d), BF16), pltpu.VMEM((tm, d), F32)],
        compiler_params=_params(("parallel", "arbitrary")),
        name="mlp",
    )(x2, g, w1, w2, g_final)


def _pad_lanes(v, offset):
    return jnp.zeros((1, LANES), F32).at[0, offset:offset + v.shape[0]].set(v.astype(F32))


def _tile(n, pref):
    while n % pref:
        pref //= 2
    return pref


def kernel(x, mem, norm_mix_g, w_in, gdn_conv_w, gdn_a_log, gdn_dt_bias, gdn_norm_g, gla_w_gate2, gla_b_gate, gla_norm_g, w_branch_gdn, w_branch_gla, w_out, norm_xattn_g, norm_mem_g, xattn_wq, xattn_wk, xattn_wv, xattn_wo, norm_mlp_g, mlp_w1, mlp_w2, norm_final_g):
    bsz, seq, d = x.shape
    n_mem = mem.shape[1]
    depth = w_in.shape[0]
    gdn_heads = gdn_a_log.shape[1]
    gdn_hd = gdn_norm_g.shape[1]
    gdn_w = gdn_heads * gdn_hd
    gla_kw = gla_b_gate.shape[1]
    gla_dv = gla_norm_g.shape[1]
    gla_vw = w_branch_gla.shape[1]
    gla_heads = gla_vw // gla_dv
    gla_dk = gla_kw // gla_heads
    rank = gla_w_gate2.shape[1]
    assert depth >= 1 and 2 * gdn_heads + rank <= LANES and seq % CHUNK == 0

    sizes = (gdn_w, gdn_w, gdn_w, gdn_w, gdn_heads, gdn_heads, gla_kw, gla_kw, gla_vw, gla_vw, rank, d, d)
    offs = [0]
    for s in sizes:
        offs.append(offs[-1] + s)
    assert offs[-1] == w_in.shape[2]
    big_cols = [0, 1, 2, 3, 6, 7, 8, 9, 11, 12]
    gla_col0 = 4 * gdn_w
    gate_col = gla_col0 + 2 * gla_kw + 2 * gla_vw

    t = bsz * seq
    x2 = x.reshape(t, d).astype(F32)
    mem2 = mem.reshape(bsz * n_mem, d).astype(F32)
    row = lambda v: v.reshape(1, -1).astype(F32)
    tm = _tile(t, 1024)
    ts = _tile(seq, 512)

    for i in range(depth):
        wi = w_in[i]
        w_big = jnp.concatenate([wi[:, offs[c]:offs[c + 1]] for c in big_cols], axis=1).astype(BF16)
        w_small = jnp.zeros((d, LANES), F32)
        w_small = w_small.at[:, 0:gdn_heads].set(wi[:, offs[4]:offs[5]])
        w_small = w_small.at[:, gdn_heads:2 * gdn_heads].set(wi[:, offs[5]:offs[6]])
        w_small = w_small.at[:, 2 * gdn_heads:2 * gdn_heads + rank].set(wi[:, offs[10]:offs[11]])
        w_gate = jnp.zeros((LANES, gla_kw), F32).at[2 * gdn_heads:2 * gdn_heads + rank].set(gla_w_gate2[i])

        proj, small = _inproj(x2, row(norm_mix_g[i]), w_big, w_small.astype(BF16), tm=tm, tn=_tile(w_big.shape[1], 2304))
        o_a = _gdn(proj, small, gdn_conv_w[i].astype(F32), _pad_lanes(gdn_a_log[i], 0),
                   _pad_lanes(gdn_dt_bias[i], 0), row(gdn_norm_g[i]),
                   bsz=bsz, seq=seq, ts=ts, nh=gdn_heads, hd=gdn_hd)
        o_b = _gla(proj, small, w_gate.astype(BF16), row(gla_b_gate[i]), row(gla_norm_g[i]),
                   bsz=bsz, seq=seq, ts=ts, nh=gla_heads, dk=gla_dk, dv=gla_dv, col0=gla_col0)
        x2 = _merge(x2, o_a, o_b, proj, w_branch_gdn[i].astype(BF16), w_branch_gla[i].astype(BF16),
                    w_out[i].astype(BF16), tm=_tile(t, 1024), gate_col=gate_col)

        k_mem, v_mem = _memkv(mem2, row(norm_mem_g[i]), xattn_wk[i].astype(BF16), xattn_wv[i].astype(BF16), n_mem=n_mem)
        x2 = _xattn(x2, row(norm_xattn_g[i]), k_mem, v_mem, xattn_wq[i].astype(BF16), xattn_wo[i].astype(BF16),
                    bsz=bsz, seq=seq, tm=_tile(seq, 1024), n_mem=n_mem)

        last = i == depth - 1
        x2 = _mlp(x2, row(norm_mlp_g[i]), mlp_w1[i].astype(BF16), mlp_w2[i].astype(BF16), row(norm_final_g),
                  tm=tm, tf=_tile(mlp_w1.shape[2], 2048), final_norm=last)
    return x2.reshape(bsz, seq, d).astype(x.dtype)
```

```python
import functools

import jax
import jax.numpy as jnp
from jax import lax
from jax.experimental import pallas as pl
from jax.experimental.pallas import tpu as pltpu

F32 = jnp.float32
BF16 = jnp.bfloat16

CHUNK = 64
CONV_TAPS = 4
GLA_GATE_TAU = 16.0
XATTN_HEADS = 4
NORM_EPS = 1e-6
NEG_LOG2E = -1.4426950408889634
LANES = 128
CARRY_ROWS = 8
INV_LEAF = 16
PACK = 256
CONV_ROWS = 256
INPROJ_STEPS = 4

VMEM_LIMIT = 48 * 1024 * 1024


def _params(semantics):
    return pltpu.CompilerParams(dimension_semantics=semantics, vmem_limit_bytes=VMEM_LIMIT)


def _dot(a, b):
    return jnp.dot(a.astype(BF16), b.astype(BF16), preferred_element_type=F32)


def _dot_nt(a, b):
    return lax.dot_general(a.astype(BF16), b.astype(BF16), (((1,), (1,)), ((), ())),
                           preferred_element_type=F32)


def _split(a):
    hi = a.astype(BF16)
    lo = (a - hi.astype(F32)).astype(BF16)
    return hi, lo


def _dot_exact_lhs(tri_bf16, b):
    bh, bl = _split(b)
    d = functools.partial(jnp.dot, preferred_element_type=F32)
    return d(tri_bf16, bh) + d(tri_bf16, bl)


def _sigmoid(x):
    return 1.0 / (1.0 + jnp.exp2(x * NEG_LOG2E))


def _softplus(x):
    return jnp.maximum(x, 0.0) + jnp.log(1.0 + jnp.exp(-jnp.abs(x)))


def _rms(x, g):
    return x * lax.rsqrt(jnp.mean(x * x, axis=-1, keepdims=True) + NORM_EPS) * g


def _chunk_cumsum_matrix(ts):
    r = lax.broadcasted_iota(jnp.int32, (ts, ts), 0)
    c = lax.broadcasted_iota(jnp.int32, (ts, ts), 1)
    return jnp.where((r >= c) & ((r // CHUNK) == (c // CHUNK)), 1.0, 0.0).astype(BF16)


def _inproj_kernel(x_ref, g_ref, w_ref, wsm_ref, cw_ref, qkv_ref, rest_ref, sm_ref, h_scr, carry,
                   *, tiles_per_seq, ncv, hd):
    i = pl.program_id(0)
    j = pl.program_id(1)
    tm = x_ref.shape[0]

    @pl.when(j == 0)
    def _():
        h = _rms(x_ref[...], g_ref[...]).astype(BF16)
        h_scr[...] = h
        sm_ref[...] = jnp.dot(h, wsm_ref[...], preferred_element_type=F32)

    @pl.when(i % tiles_per_seq == 0)
    def _():
        carry[j] = jnp.zeros((CARRY_ROWS, ncv), F32)

    per_type = ncv // (3 * hd)
    cw = min(PACK, per_type * hd)
    nslots = ncv // cw
    rw = rest_ref.shape[1] // nslots
    sub = lax.broadcasted_iota(jnp.int32, (CARRY_ROWS, cw), 0)
    for slot in range(nslots):
        cs = slice(slot * cw, (slot + 1) * cw)
        x = jnp.dot(h_scr[...], w_ref[:, cs], preferred_element_type=F32)
        prev = carry[j, :, cs]
        carry[j, :, cs] = x[tm - CARRY_ROWS:]
        kind = (slot * cw) // (per_type * hd)
        for rb in range(tm // CONV_ROWS):
            r0 = rb * CONV_ROWS
            xb = x[r0:r0 + CONV_ROWS]
            if rb:
                prev = x[r0 - CARRY_ROWS:r0]
            y = cw_ref[CONV_TAPS - 1:CONV_TAPS, cs] * xb
            for sft in range(1, CONV_TAPS):
                r = pltpu.roll(xb, sft, 0)
                head_rows = jnp.where(sub < sft, pltpu.roll(prev, sft, 0), r[:CARRY_ROWS])
                r = jnp.concatenate([head_rows, r[CARRY_ROWS:]], axis=0)
                y = y + cw_ref[CONV_TAPS - 1 - sft:CONV_TAPS - sft, cs] * r
            y = y * _sigmoid(y)
            for hh in range(cw // hd):
                yh = y[:, hh * hd:(hh + 1) * hd]
                if kind < 2:
                    yh = yh * (lax.rsqrt(jnp.sum(yh * yh, axis=-1, keepdims=True) + NORM_EPS)
                               * (hd ** -0.5 if kind == 0 else 1.0))
                qkv_ref[r0:r0 + CONV_ROWS, slot * cw + hh * hd:slot * cw + (hh + 1) * hd] = yh.astype(qkv_ref.dtype)
        rs = slice(slot * rw, (slot + 1) * rw)
        rest_ref[:, rs] = jnp.dot(h_scr[...], w_ref[:, ncv + slot * rw:ncv + (slot + 1) * rw],
                                  preferred_element_type=F32).astype(rest_ref.dtype)


def _inproj(x2, g, w_steps, w_small, conv_w, *, tm, nsteps, ncv, hd, tiles_per_seq):
    t, d = x2.shape
    tn = w_steps.shape[1] // nsteps
    nrest = tn - ncv
    return pl.pallas_call(
        functools.partial(_inproj_kernel, tiles_per_seq=tiles_per_seq, ncv=ncv, hd=hd),
        grid=(t // tm, nsteps),
        in_specs=[
            pl.BlockSpec((tm, d), lambda i, j: (i, 0)),
            pl.BlockSpec((1, d), lambda i, j: (0, 0)),
            pl.BlockSpec((d, tn), lambda i, j: (0, j)),
            pl.BlockSpec((d, LANES), lambda i, j: (0, 0)),
            pl.BlockSpec((CONV_TAPS, ncv), lambda i, j: (0, j)),
        ],
        out_specs=[
            pl.BlockSpec((tm, ncv), lambda i, j: (i, j)),
            pl.BlockSpec((tm, nrest), lambda i, j: (i, j)),
            pl.BlockSpec((tm, LANES), lambda i, j: (i, 0)),
        ],
        out_shape=[
            jax.ShapeDtypeStruct((t, nsteps * ncv), BF16),
            jax.ShapeDtypeStruct((t, nsteps * nrest), BF16),
            jax.ShapeDtypeStruct((t, LANES), F32),
        ],
        scratch_shapes=[pltpu.VMEM((tm, d), BF16), pltpu.VMEM((nsteps, CARRY_ROWS, ncv), F32)],
        compiler_params=_params(("arbitrary", "arbitrary")),
        name="inproj",
    )(x2, g, w_steps, w_small, conv_w)


def _mask01(shape, fn):
    r = lax.broadcasted_iota(jnp.int32, shape, 0)
    c = lax.broadcasted_iota(jnp.int32, shape, 1)
    return fn(r, c)


def _bf01(shape, fn):
    return jnp.where(_mask01(shape, fn), 1.0, 0.0).astype(BF16)


def _dot_exact_rhs(a, e_bf16):
    ah, al = _split(a)
    d = functools.partial(jnp.dot, preferred_element_type=F32)
    return d(ah, e_bf16) + d(al, e_bf16)


def _mm_packed(lhs, bp, bd01):
    reps = PACK // CHUNK
    bd = jnp.concatenate([bp.astype(BF16)] * reps, axis=0) * bd01
    return jnp.dot(lhs.astype(BF16), bd, preferred_element_type=F32)


def _packed_unit_lower_inverse(lows, eye_p, leaf_p, bd01):
    mm = functools.partial(_mm_packed, bd01=bd01)
    stack = lambda a, b: jnp.concatenate([a, b], axis=0)
    dg = [jnp.where(leaf_p, l, 0.0) for l in lows]
    off = [l - d for l, d in zip(lows, dg)]
    d2 = [mm(d, d) for d in dg]
    p = [eye_p - d for d in dg]
    r = [mm(stack(pi, di), di) for pi, di in zip(p, d2)]
    p = [pi + ri[:CHUNK] for pi, ri in zip(p, r)]
    d4 = [ri[CHUNK:] for ri in r]
    r = [mm(stack(pi, di), di) for pi, di in zip(p, d4)]
    p = [pi + ri[:CHUNK] for pi, ri in zip(p, r)]
    d8 = [ri[CHUNK:] for ri in r]
    p = [pi + mm(pi, di) for pi, di in zip(p, d8)]
    n = [mm(pi, oi) for pi, oi in zip(p, off)]
    n2 = [mm(ni, ni) for ni in n]
    rr = [eye_p - ni for ni in n]
    rr = [ri + mm(ri, ni) for ri, ni in zip(rr, n2)]
    return [mm(ri, pi) for ri, pi in zip(rr, p)]


def _gdn_kernel(qkv_ref, z_ref, sm_ref, alog_ref, dtb_ref, ng_ref, o_ref,
                gcw_scr, btw_scr, w2_scr, bm_scr, qp_scr, au_scr, ee_scr, s_scr, *, ts, nh, hd, hps):
    w = nh * hd
    nch = ts // CHUNK
    gh = PACK // CHUNK
    ngrp = nh // gh
    ph = PACK // hd
    npair = nh // ph

    @pl.when(pl.program_id(1) == 0)
    def _():
        s_scr[...] = jnp.zeros_like(s_scr)

    def head(kind, rows, h):
        c0 = (h // hps) * (3 * hps * hd) + kind * hps * hd + (h % hps) * hd
        return qkv_ref[rows, c0:c0 + hd]

    sm = sm_ref[...]
    g = -jnp.exp(alog_ref[...]) * _softplus(sm + dtb_ref[...])
    gc = _dot_exact_lhs(_chunk_cumsum_matrix(ts), g)
    beta = _sigmoid(sm)
    gcw_scr[...] = _dot_exact_rhs(gc, _bf01((LANES, w), lambda r, c: r == c // hd))
    btw_scr[...] = _dot_exact_rhs(beta, _bf01((LANES, w), lambda r, c: r == nh + c // hd))

    tile_p = (ts, PACK)
    incl_p = _mask01(tile_p, lambda r, c: r % CHUNK >= c % CHUNK)
    diag_p = _mask01(tile_p, lambda r, c: r % CHUNK == c % CHUNK)
    same_chunk = _bf01((ts, ts), lambda r, c: r // CHUNK == c // CHUNK)
    chunk_p = (CHUNK, PACK)
    strict_p = _mask01(chunk_p, lambda r, c: r > c % CHUNK)
    leaf_p = _mask01(chunk_p, lambda r, c: r // INV_LEAF == (c % CHUNK) // INV_LEAF)
    eye_p = jnp.where(_mask01(chunk_p, lambda r, c: r == c % CHUNK), 1.0, 0.0).astype(F32)
    head_p = [_mask01(chunk_p, lambda r, c, hh=hh: c // CHUNK == hh) for hh in range(gh)]
    bd01 = _bf01((PACK, PACK), lambda r, c: r // CHUNK == c // CHUNK)
    kbd01 = _bf01((PACK, gh * hd), lambda r, c: r // CHUNK == c // hd)

    dec_g, bcol_g = [], []
    for gi in range(ngrp):
        gcol = _dot_exact_rhs(gc, _bf01((LANES, PACK), lambda r, c, gi=gi: r == gi * gh + c // CHUNK))
        grow = _dot_exact_lhs(same_chunk, jnp.where(diag_p, gcol, 0.0))
        dec_g.append(jnp.exp(jnp.where(incl_p, gcol - grow, -jnp.inf)))
        bcol_g.append(_dot_exact_rhs(beta, _bf01((LANES, PACK), lambda r, c, gi=gi: r == nh + gi * gh + c // CHUNK)))

    chains = [(c, gi) for c in range(nch) for gi in range(ngrp)]
    lows, a_ps = [], []
    for c, gi in chains:
        rows = slice(c * CHUNK, (c + 1) * CHUNK)
        qg = jnp.concatenate([head(0, rows, gi * gh + hh) for hh in range(gh)], axis=1)
        kg = jnp.concatenate([head(1, rows, gi * gh + hh) for hh in range(gh)], axis=1)
        kbd = jnp.concatenate([kg] * gh, axis=0) * kbd01
        qk_kk = lax.dot_general(jnp.concatenate([qg, kg], axis=0), kbd, (((1,), (1,)), ((), ())),
                                preferred_element_type=F32)
        dec = dec_g[gi][rows]
        lows.append(jnp.where(strict_p, bcol_g[gi][rows] * qk_kk[CHUNK:] * dec, 0.0))
        a_ps.append(qk_kk[:CHUNK] * dec)

    tinvs = _packed_unit_lower_inverse(lows, eye_p, leaf_p, bd01)

    def blockrows(mat_p):
        return jnp.concatenate([jnp.where(m, mat_p, 0.0) for m in head_p], axis=0).astype(BF16)

    uws = []
    for (c, gi), tinv in zip(chains, tinvs):
        rows = slice(c * CHUNK, (c + 1) * CHUNK)
        rhs = []
        for hh in range(gh):
            h = gi * gh + hh
            hc = slice(h * hd, (h + 1) * hd)
            bt = btw_scr[rows, hc]
            rhs.append(jnp.concatenate([bt * head(2, rows, h).astype(F32),
                                        bt * jnp.exp(gcw_scr[rows, hc]) * head(1, rows, h).astype(F32)], axis=1))
        uws.append(_dot(blockrows(tinv), jnp.concatenate(rhs, axis=0)))
    auws = [_dot(blockrows(a_p), uw) for a_p, uw in zip(a_ps, uws)]

    for (c, gi), uw, auw in zip(chains, uws, auws):
        rows = slice(c * CHUNK, (c + 1) * CHUNK)
        last = slice((c + 1) * CHUNK - 1, (c + 1) * CHUNK)
        for hh in range(gh):
            h = gi * gh + hh
            hc = slice(h * hd, (h + 1) * hd)
            hr = slice(hh * CHUNK, (hh + 1) * CHUNK)
            gcb = gcw_scr[rows, hc]
            au_scr[rows, hc] = auw[hr, :hd]
            qp_scr[rows, hc] = (head(0, rows, h).astype(F32) * jnp.exp(gcb) - auw[hr, hd:]).astype(qp_scr.dtype)
            kd_t = (head(1, rows, h).astype(F32) * jnp.exp(gcw_scr[last, hc] - gcb)).T
            bw = _dot(kd_t, uw[hr])
            sc = slice((h % ph) * hd, (h % ph + 1) * hd)
            bm_scr[c, h // ph, :, sc] = bw[:, :hd]
            w2_scr[c, h // ph, :, sc] = bw[:, hd:].astype(w2_scr.dtype)
    for c in range(nch):
        last = slice((c + 1) * CHUNK - 1, (c + 1) * CHUNK)
        ee_scr[c * CARRY_ROWS:(c + 1) * CARRY_ROWS, :] = jnp.broadcast_to(jnp.exp(gcw_scr[last, :]), (CARRY_ROWS, w))

    left = _mask01((hd, PACK), lambda r, c: c < hd)
    ng = ng_ref[...]

    states = [s_scr[p] for p in range(npair)]
    for c in range(nch):
        rows = slice(c * CHUNK, (c + 1) * CHUNK)
        ee = ee_scr[c * CARRY_ROWS:c * CARRY_ROWS + 1, :]
        prods = []
        for p in range(npair):
            pc = slice(p * PACK, (p + 1) * PACK)
            s = states[p]
            s_bd = jnp.concatenate([jnp.where(left, s, 0.0), jnp.where(left, 0.0, s)], axis=0).astype(BF16)
            lhs = jnp.concatenate([w2_scr[c, p], qp_scr[rows, pc]], axis=0)
            prods.append(jnp.dot(lhs, s_bd, preferred_element_type=F32))
        for p, r in enumerate(prods):
            pc = slice(p * PACK, (p + 1) * PACK)
            states[p] = ee[:, pc] * states[p] + bm_scr[c, p] - r[:hd]
            o = au_scr[rows, pc] + r[hd:]
            for side in range(ph):
                hc = slice((p * ph + side) * hd, (p * ph + side + 1) * hd)
                z = z_ref[rows, hc].astype(F32)
                o_ref[rows, hc] = (_rms(o[:, side * hd:(side + 1) * hd], ng) * (z * _sigmoid(z))).astype(o_ref.dtype)
    for p in range(npair):
        s_scr[p] = states[p]


def _gdn(qkv, rest, small, alog, dtb, norm_g, *, bsz, seq, ts, nh, hd, hps):
    t = bsz * seq
    nt = seq // ts
    w = nh * hd
    nch = ts // CHUNK
    assert PACK % CHUNK == 0 and PACK % hd == 0 and nh % (PACK // CHUNK) == 0 and nh % (PACK // hd) == 0
    full = lambda shape: pl.BlockSpec(shape, lambda b, j: (0, 0))
    return pl.pallas_call(
        functools.partial(_gdn_kernel, ts=ts, nh=nh, hd=hd, hps=hps),
        grid=(bsz, nt),
        in_specs=[pl.BlockSpec((ts, 3 * w), lambda b, j: (b * nt + j, 0)),
                  pl.BlockSpec((ts, w), lambda b, j: (b * nt + j, 0)),
                  pl.BlockSpec((ts, LANES), lambda b, j: (b * nt + j, 0)),
                  full((1, LANES)), full((1, LANES)), full((1, hd))],
        out_specs=pl.BlockSpec((ts, w), lambda b, j: (b * nt + j, 0)),
        out_shape=jax.ShapeDtypeStruct((t, w), BF16),
        scratch_shapes=[
            pltpu.VMEM((ts, w), F32), pltpu.VMEM((ts, w), F32),
            pltpu.VMEM((nch, nh * hd // PACK, hd, PACK), BF16),
            pltpu.VMEM((nch, nh * hd // PACK, hd, PACK), F32),
            pltpu.VMEM((ts, w), BF16), pltpu.VMEM((ts, w), F32),
            pltpu.VMEM((nch * CARRY_ROWS, w), F32),
            pltpu.VMEM((nh * hd // PACK, hd, PACK), F32),
        ],
        compiler_params=_params(("parallel", "arbitrary")),
        name="gdn",
    )(qkv, rest, small, alog, dtb, norm_g)


def _gla_kernel(q_ref, k_ref, v_ref, r_ref, sm_ref, wg_ref, bg_ref, ng_ref, o_ref, st_scr, *, ts, nh, dk, dv):
    nch = ts // CHUNK
    gh = PACK // CHUNK
    ngrp = nh // gh

    @pl.when(pl.program_id(1) == 0)
    def _():
        st_scr[...] = jnp.zeros_like(st_scr)

    lg = _dot(sm_ref[...], wg_ref[...]) + bg_ref[...]
    log_fg = (jnp.minimum(lg, 0.0) - jnp.log(1.0 + jnp.exp(-jnp.abs(lg)))) / GLA_GATE_TAU
    bc_all = _dot_exact_lhs(_chunk_cumsum_matrix(ts), log_fg)

    chunk_p = (CHUNK, PACK)
    incl_p = _mask01(chunk_p, lambda r, c: r >= c % CHUNK)
    head_p = [_mask01(chunk_p, lambda r, c, hh=hh: c // CHUNK == hh) for hh in range(gh)]
    kbd01 = _bf01((PACK, gh * dk), lambda r, c: r // CHUNK == c // dk)
    ng = ng_ref[...]
    scale = dk ** -0.5

    def blockrows(mat_p):
        return jnp.concatenate([jnp.where(m, mat_p, 0.0) for m in head_p], axis=0).astype(BF16)

    qd, kd, ge, a_ps = [], [], [], []
    for c in range(nch):
        rows = slice(c * CHUNK, (c + 1) * CHUNK)
        bc = bc_all[rows]
        bref = bc[CHUNK // 2:CHUNK // 2 + 1, :]
        bend = bc[CHUNK - 1:CHUNK, :]
        q = q_ref[rows, :].astype(F32) * scale
        k = k_ref[rows, :].astype(F32)
        qe = (q * jnp.exp(bc - bref)).astype(BF16)
        ke = (k * jnp.exp(bref - bc)).astype(BF16)
        qd.append((q * jnp.exp(bc)).astype(BF16))
        kd.append((k * jnp.exp(bend - bc)).astype(BF16))
        ge.append(jnp.exp(bend))
        for gi in range(ngrp):
            gcols = slice(gi * gh * dk, (gi + 1) * gh * dk)
            kebd = jnp.concatenate([ke[:, gcols]] * gh, axis=0) * kbd01
            a = lax.dot_general(qe[:, gcols], kebd, (((1,), (1,)), ((), ())), preferred_element_type=F32)
            a_ps.append(jnp.where(incl_p, a, 0.0))

    oi = []
    for c in range(nch):
        rows = slice(c * CHUNK, (c + 1) * CHUNK)
        for gi in range(ngrp):
            v4 = jnp.concatenate([v_ref[rows, (gi * gh + hh) * dv:(gi * gh + hh + 1) * dv] for hh in range(gh)], axis=0)
            oi.append(jnp.dot(blockrows(a_ps[c * ngrp + gi]), v4, preferred_element_type=F32))
    kvt = []
    for c in range(nch):
        rows = slice(c * CHUNK, (c + 1) * CHUNK)
        for h in range(nh):
            v_t = v_ref[rows, h * dv:(h + 1) * dv].astype(F32).T
            kvt.append(_dot(v_t, kd[c][:, h * dk:(h + 1) * dk]))

    st = [st_scr[h] for h in range(nh)]
    for c in range(nch):
        rows = slice(c * CHUNK, (c + 1) * CHUNK)
        for h in range(nh):
            ks = slice(h * dk, (h + 1) * dk)
            vs = slice(h * dv, (h + 1) * dv)
            hr = slice((h % gh) * CHUNK, (h % gh + 1) * CHUNK)
            o = _dot_nt(qd[c][:, ks], st[h]) + oi[c * ngrp + h // gh][hr]
            st[h] = ge[c][:, ks] * st[h] + kvt[c * nh + h]
            r = r_ref[rows, vs].astype(F32)
            o_ref[rows, vs] = (_rms(o, ng) * (r * _sigmoid(r))).astype(o_ref.dtype)
    for h in range(nh):
        st_scr[h] = st[h]


def _gla(rest, small, w_gate, b_gate, norm_g, *, bsz, seq, ts, nh, dk, dv, col0):
    t = bsz * seq
    nt = seq // ts
    kw, vw = nh * dk, nh * dv
    assert nh % (PACK // CHUNK) == 0
    spec = lambda width, start: pl.BlockSpec((ts, width), lambda b, j: (b * nt + j, start // width))
    full = lambda shape: pl.BlockSpec(shape, lambda b, j: (0, 0))
    return pl.pallas_call(
        functools.partial(_gla_kernel, ts=ts, nh=nh, dk=dk, dv=dv),
        grid=(bsz, nt),
        in_specs=[spec(kw, col0), spec(kw, col0 + kw), spec(vw, col0 + 2 * kw), spec(vw, col0 + 2 * kw + vw),
                  pl.BlockSpec((ts, LANES), lambda b, j: (b * nt + j, 0)),
                  full((LANES, kw)), full((1, kw)), full((1, dv))],
        out_specs=pl.BlockSpec((ts, vw), lambda b, j: (b * nt + j, 0)),
        out_shape=jax.ShapeDtypeStruct((t, vw), BF16),
        scratch_shapes=[pltpu.VMEM((nh, dv, dk), F32)],
        compiler_params=_params(("parallel", "arbitrary")),
        name="gla",
    )(rest, rest, rest, rest, small, w_gate, b_gate, norm_g)


def _merge_kernel(x_ref, oa_ref, ob_ref, ga_ref, gb_ref, wa_ref, wb_ref, wo_ref, out_ref):
    ya = jnp.dot(oa_ref[...], wa_ref[...], preferred_element_type=F32)
    yb = jnp.dot(ob_ref[...], wb_ref[...], preferred_element_type=F32)
    merged = _sigmoid(ga_ref[...].astype(F32)) * ya + _sigmoid(gb_ref[...].astype(F32)) * yb
    out_ref[...] = x_ref[...] + _dot(merged, wo_ref[...])


def _merge(x2, o_a, o_b, rest, w_a, w_b, w_o, *, tm, gate_col):
    t, d = x2.shape
    tok = lambda cb: pl.BlockSpec((tm, d), lambda i: (i, cb))
    wsp = pl.BlockSpec((d, d), lambda i: (0, 0))
    return pl.pallas_call(
        _merge_kernel,
        grid=(t // tm,),
        in_specs=[tok(0), tok(0), tok(0), tok(gate_col // d), tok(gate_col // d + 1), wsp, wsp, wsp],
        out_specs=tok(0),
        out_shape=jax.ShapeDtypeStruct((t, d), F32),
        compiler_params=_params(("parallel",)),
        name="merge",
    )(x2, o_a, o_b, rest, rest, w_a, w_b, w_o)


def _memkv_kernel(m_ref, g_ref, wk_ref, wv_ref, k_ref, v_ref):
    m = _rms(m_ref[...], g_ref[...]).astype(BF16)
    k_ref[...] = jnp.dot(m, wk_ref[...], preferred_element_type=F32).astype(BF16)
    v_ref[...] = jnp.dot(m, wv_ref[...], preferred_element_type=F32).astype(BF16)


def _memkv(mem2, g, wk, wv, *, n_mem):
    t, d = mem2.shape
    tok = pl.BlockSpec((n_mem, d), lambda b: (b, 0))
    wsp = pl.BlockSpec((d, d), lambda b: (0, 0))
    return pl.pallas_call(
        _memkv_kernel,
        grid=(t // n_mem,),
        in_specs=[tok, pl.BlockSpec((1, d), lambda b: (0, 0)), wsp, wsp],
        out_specs=[tok, tok],
        out_shape=[jax.ShapeDtypeStruct((t, d), BF16)] * 2,
        compiler_params=_params(("parallel",)),
        name="memkv",
    )(mem2, g, wk, wv)


def _xattn_kernel(x_ref, g_ref, k_ref, v_ref, wq_ref, wo_ref, out_ref, *, nh):
    x = x_ref[...]
    d = x.shape[-1]
    hd = d // nh
    q = _dot(_rms(x, g_ref[...]), wq_ref[...])
    cols = [slice(h * hd, (h + 1) * hd) for h in range(nh)]
    qb = q.astype(BF16)
    scores = [_dot_nt(qb[:, cs], k_ref[:, cs]) * (hd ** -0.5) for cs in cols]
    probs = []
    for s in scores:
        e = jnp.exp(s - jnp.max(s, axis=-1, keepdims=True))
        probs.append((e / jnp.sum(e, axis=-1, keepdims=True)).astype(BF16))
    o = jnp.concatenate([jnp.dot(p, v_ref[:, cs], preferred_element_type=F32) for p, cs in zip(probs, cols)], axis=1)
    out_ref[...] = x + _dot(o, wo_ref[...])


def _xattn(x2, g, k_mem, v_mem, wq, wo, *, bsz, seq, tm, n_mem):
    t, d = x2.shape
    nt = seq // tm
    tok = pl.BlockSpec((tm, d), lambda b, j: (b * nt + j, 0))
    mem = pl.BlockSpec((n_mem, d), lambda b, j: (b, 0))
    wsp = pl.BlockSpec((d, d), lambda b, j: (0, 0))
    return pl.pallas_call(
        functools.partial(_xattn_kernel, nh=XATTN_HEADS),
        grid=(bsz, nt),
        in_specs=[tok, pl.BlockSpec((1, d), lambda b, j: (0, 0)), mem, mem, wsp, wsp],
        out_specs=tok,
        out_shape=jax.ShapeDtypeStruct((t, d), F32),
        compiler_params=_params(("parallel", "parallel")),
        name="xattn",
    )(x2, g, k_mem, v_mem, wq, wo)


def _mlp_kernel(x_ref, g_ref, w1_ref, w2_ref, gf_ref, out_ref, h_scr, acc_scr, *, final_norm):
    j = pl.program_id(1)

    @pl.when(j == 0)
    def _():
        h_scr[...] = _rms(x_ref[...], g_ref[...]).astype(BF16)
        acc_scr[...] = x_ref[...]

    a = jnp.dot(h_scr[...], w1_ref[...], preferred_element_type=F32)
    acc_scr[...] += _dot(jnp.square(jnp.maximum(a, 0.0)), w2_ref[...])

    @pl.when(j == pl.num_programs(1) - 1)
    def _():
        y = acc_scr[...]
        out_ref[...] = _rms(y, gf_ref[...]) if final_norm else y


def _mlp(x2, g, w1, w2, g_final, *, tm, tf, final_norm):
    t, d = x2.shape
    ff = w1.shape[1]
    tok = pl.BlockSpec((tm, d), lambda i, j: (i, 0))
    vec = pl.BlockSpec((1, d), lambda i, j: (0, 0))
    return pl.pallas_call(
        functools.partial(_mlp_kernel, final_norm=final_norm),
        grid=(t // tm, ff // tf),
        in_specs=[tok, vec, pl.BlockSpec((d, tf), lambda i, j: (0, j)),
                  pl.BlockSpec((tf, d), lambda i, j: (j, 0)), vec],
        out_specs=tok,
        out_shape=jax.ShapeDtypeStruct((t, d), F32),
        scratch_shapes=[pltpu.VMEM((tm, d), BF16), pltpu.VMEM((tm, d), F32)],
        compiler_params=_params(("parallel", "arbitrary")),
        name="mlp",
    )(x2, g, w1, w2, g_final)


def _pad_lanes(v, offset):
    return jnp.zeros((1, LANES), F32).at[0, offset:offset + v.shape[0]].set(v.astype(F32))


def _tile(n, pref):
    while n % pref:
        pref //= 2
    return pref


def kernel(x, mem, norm_mix_g, w_in, gdn_conv_w, gdn_a_log, gdn_dt_bias, gdn_norm_g, gla_w_gate2, gla_b_gate, gla_norm_g, w_branch_gdn, w_branch_gla, w_out, norm_xattn_g, norm_mem_g, xattn_wq, xattn_wk, xattn_wv, xattn_wo, norm_mlp_g, mlp_w1, mlp_w2, norm_final_g):
    bsz, seq, d = x.shape
    n_mem = mem.shape[1]
    depth = w_in.shape[0]
    gdn_heads = gdn_a_log.shape[1]
    gdn_hd = gdn_norm_g.shape[1]
    gdn_w = gdn_heads * gdn_hd
    gla_kw = gla_b_gate.shape[1]
    gla_dv = gla_norm_g.shape[1]
    gla_vw = w_branch_gla.shape[1]
    gla_heads = gla_vw // gla_dv
    gla_dk = gla_kw // gla_heads
    rank = gla_w_gate2.shape[1]
    assert depth >= 1 and 2 * gdn_heads + rank <= LANES and seq % CHUNK == 0

    sizes = (gdn_w, gdn_w, gdn_w, gdn_w, gdn_heads, gdn_heads, gla_kw, gla_kw, gla_vw, gla_vw, rank, d, d)
    offs = [0]
    for s in sizes:
        offs.append(offs[-1] + s)
    assert offs[-1] == w_in.shape[2]
    rest_cols = [3, 6, 7, 8, 9, 11, 12]
    gla_col0 = gdn_w
    gate_col = gla_col0 + 2 * gla_kw + 2 * gla_vw
    nsteps = INPROJ_STEPS
    hps = gdn_heads // nsteps
    ncv = 3 * hps * gdn_hd
    n_rest = sum(sizes[c] for c in rest_cols)
    assert gdn_heads % nsteps == 0 and n_rest % (nsteps * 3 * LANES) == 0 and (hps * gdn_hd) % PACK in (0, hps * gdn_hd)

    t = bsz * seq
    x2 = x.reshape(t, d).astype(F32)
    mem2 = mem.reshape(bsz * n_mem, d).astype(F32)
    row = lambda v: v.reshape(1, -1).astype(F32)
    tm = _tile(seq, 1024)
    ts = _tile(seq, 512)

    def step_order(a):
        hw = hps * gdn_hd
        return jnp.concatenate([a[:, kind * gdn_w + st * hw:kind * gdn_w + (st + 1) * hw]
                                for st in range(nsteps) for kind in range(3)], axis=1)

    for i in range(depth):
        wi = w_in[i]
        w_qkv = step_order(wi[:, :3 * gdn_w])
        w_rest = jnp.concatenate([wi[:, offs[c]:offs[c + 1]] for c in rest_cols], axis=1)
        nr = n_rest // nsteps
        w_steps = jnp.concatenate(
            [blk for st in range(nsteps) for blk in (w_qkv[:, st * ncv:(st + 1) * ncv], w_rest[:, st * nr:(st + 1) * nr])],
            axis=1).astype(BF16)
        w_small = jnp.zeros((d, LANES), F32)
        w_small = w_small.at[:, 0:gdn_heads].set(wi[:, offs[4]:offs[5]])
        w_small = w_small.at[:, gdn_heads:2 * gdn_heads].set(wi[:, offs[5]:offs[6]])
        w_small = w_small.at[:, 2 * gdn_heads:2 * gdn_heads + rank].set(wi[:, offs[10]:offs[11]])
        w_gate = jnp.zeros((LANES, gla_kw), F32).at[2 * gdn_heads:2 * gdn_heads + rank].set(gla_w_gate2[i])
        conv_w = step_order(gdn_conv_w[i].astype(F32))

        qkv, rest, small = _inproj(x2, row(norm_mix_g[i]), w_steps, w_small.astype(BF16), conv_w,
                                   tm=tm, nsteps=nsteps, ncv=ncv, hd=gdn_hd, tiles_per_seq=seq // tm)
        o_a = _gdn(qkv, rest, small, _pad_lanes(gdn_a_log[i], 0), _pad_lanes(gdn_dt_bias[i], 0), row(gdn_norm_g[i]),
                   bsz=bsz, seq=seq, ts=ts, nh=gdn_heads, hd=gdn_hd, hps=hps)
        o_b = _gla(rest, small, w_gate.astype(BF16), row(gla_b_gate[i]), row(gla_norm_g[i]),
                   bsz=bsz, seq=seq, ts=ts, nh=gla_heads, dk=gla_dk, dv=gla_dv, col0=gla_col0)
        x2 = _merge(x2, o_a, o_b, rest, w_branch_gdn[i].astype(BF16), w_branch_gla[i].astype(BF16),
                    w_out[i].astype(BF16), tm=_tile(t, 1024), gate_col=gate_col)

        k_mem, v_mem = _memkv(mem2, row(norm_mem_g[i]), xattn_wk[i].astype(BF16), xattn_wv[i].astype(BF16), n_mem=n_mem)
        x2 = _xattn(x2, row(norm_xattn_g[i]), k_mem, v_mem, xattn_wq[i].astype(BF16), xattn_wo[i].astype(BF16),
                    bsz=bsz, seq=seq, tm=_tile(seq, 1024), n_mem=n_mem)

        last = i == depth - 1
        x2 = _mlp(x2, row(norm_mlp_g[i]), mlp_w1[i].astype(BF16), mlp_w2[i].astype(BF16), row(norm_final_g),
                  tm=_tile(t, 1024), tf=_tile(mlp_w1.shape[2], 2048), final_norm=last)
    return x2.reshape(bsz, seq, d).astype(x.dtype)
```

```python
import functools

import jax
import jax.numpy as jnp
from jax import lax
from jax.experimental import pallas as pl
from jax.experimental.pallas import tpu as pltpu

F32 = jnp.float32
BF16 = jnp.bfloat16

CHUNK = 64
CONV_TAPS = 4
GLA_GATE_TAU = 16.0
XATTN_HEADS = 4
NORM_EPS = 1e-6
NEG_LOG2E = -1.4426950408889634
LANES = 128
CARRY_ROWS = 8
INV_LEAF = 16
PACK = 256

VMEM_LIMIT = 48 * 1024 * 1024
VMEM_LIMIT_FUSED = 58 * 1024 * 1024


def _params(semantics, vmem_limit=VMEM_LIMIT):
    return pltpu.CompilerParams(dimension_semantics=semantics, vmem_limit_bytes=vmem_limit)


def _dot(a, b):
    return jnp.dot(a.astype(BF16), b.astype(BF16), preferred_element_type=F32)


def _dot_nt(a, b):
    return lax.dot_general(a.astype(BF16), b.astype(BF16), (((1,), (1,)), ((), ())),
                           preferred_element_type=F32)


def _split(a):
    hi = a.astype(BF16)
    lo = (a - hi.astype(F32)).astype(BF16)
    return hi, lo


def _dot_exact_lhs(tri_bf16, b):
    bh, bl = _split(b)
    d = functools.partial(jnp.dot, preferred_element_type=F32)
    return d(tri_bf16, bh) + d(tri_bf16, bl)


def _sigmoid(x):
    return 1.0 / (1.0 + jnp.exp2(x * NEG_LOG2E))


def _softplus(x):
    return jnp.maximum(x, 0.0) + jnp.log(1.0 + jnp.exp(-jnp.abs(x)))


def _rms(x, g):
    return x * lax.rsqrt(jnp.mean(x * x, axis=-1, keepdims=True) + NORM_EPS) * g


def _chunk_cumsum_matrix(ts):
    r = lax.broadcasted_iota(jnp.int32, (ts, ts), 0)
    c = lax.broadcasted_iota(jnp.int32, (ts, ts), 1)
    return jnp.where((r >= c) & ((r // CHUNK) == (c // CHUNK)), 1.0, 0.0).astype(BF16)


def _inproj_kernel(x_ref, g_ref, wbig_ref, wsm_ref, big_ref, sm_ref, h_scr):
    @pl.when(pl.program_id(1) == 0)
    def _():
        h = _rms(x_ref[...], g_ref[...]).astype(BF16)
        h_scr[...] = h
        sm_ref[...] = jnp.dot(h, wsm_ref[...], preferred_element_type=F32)

    big_ref[...] = jnp.dot(h_scr[...], wbig_ref[...], preferred_element_type=F32).astype(big_ref.dtype)


def _inproj(x2, g, w_big, w_small, *, tm, tn):
    t, d = x2.shape
    n = w_big.shape[1]
    return pl.pallas_call(
        _inproj_kernel,
        grid=(t // tm, n // tn),
        in_specs=[
            pl.BlockSpec((tm, d), lambda i, j: (i, 0)),
            pl.BlockSpec((1, d), lambda i, j: (0, 0)),
            pl.BlockSpec((d, tn), lambda i, j: (0, j)),
            pl.BlockSpec((d, LANES), lambda i, j: (0, 0)),
        ],
        out_specs=[
            pl.BlockSpec((tm, tn), lambda i, j: (i, j)),
            pl.BlockSpec((tm, LANES), lambda i, j: (i, 0)),
        ],
        out_shape=[
            jax.ShapeDtypeStruct((t, n), BF16),
            jax.ShapeDtypeStruct((t, LANES), F32),
        ],
        scratch_shapes=[pltpu.VMEM((tm, d), BF16)],
        compiler_params=_params(("parallel", "arbitrary")),
        name="inproj",
    )(x2, g, w_big, w_small)


def _mask01(shape, fn):
    r = lax.broadcasted_iota(jnp.int32, shape, 0)
    c = lax.broadcasted_iota(jnp.int32, shape, 1)
    return fn(r, c)


def _bf01(shape, fn):
    return jnp.where(_mask01(shape, fn), 1.0, 0.0).astype(BF16)


def _dot_exact_rhs(a, e_bf16):
    ah, al = _split(a)
    d = functools.partial(jnp.dot, preferred_element_type=F32)
    return d(ah, e_bf16) + d(al, e_bf16)


def _mm_packed(lhs, bp, bd01):
    reps = PACK // CHUNK
    bd = jnp.concatenate([bp.astype(BF16)] * reps, axis=0) * bd01
    return jnp.dot(lhs.astype(BF16), bd, preferred_element_type=F32)


def _packed_unit_lower_inverse(lows, eye_p, leaf_p, bd01):
    mm = functools.partial(_mm_packed, bd01=bd01)
    stack = lambda a, b: jnp.concatenate([a, b], axis=0)
    dg = [jnp.where(leaf_p, l, 0.0) for l in lows]
    off = [l - d for l, d in zip(lows, dg)]
    d2 = [mm(d, d) for d in dg]
    p = [eye_p - d for d in dg]
    r = [mm(stack(pi, di), di) for pi, di in zip(p, d2)]
    p = [pi + ri[:CHUNK] for pi, ri in zip(p, r)]
    d4 = [ri[CHUNK:] for ri in r]
    r = [mm(stack(pi, di), di) for pi, di in zip(p, d4)]
    p = [pi + ri[:CHUNK] for pi, ri in zip(p, r)]
    d8 = [ri[CHUNK:] for ri in r]
    p = [pi + mm(pi, di) for pi, di in zip(p, d8)]
    n = [mm(pi, oi) for pi, oi in zip(p, off)]
    n2 = [mm(ni, ni) for ni in n]
    rr = [eye_p - ni for ni in n]
    rr = [ri + mm(ri, ni) for ri, ni in zip(rr, n2)]
    return [mm(ri, pi) for ri, pi in zip(rr, p)]


def _gdn_kernel(q_ref, k_ref, v_ref, z_ref, sm_ref, cw_ref, alog_ref, dtb_ref, ng_ref, o_ref,
                xbuf, qn_scr, kn_scr, vv_scr, gcw_scr, btw_scr, w2_scr, bm_scr, qp_scr, au_scr, ee_scr,
                s_scr, *, ts, nh, hd):
    w = nh * hd
    nch = ts // CHUNK
    gh = PACK // CHUNK
    ngrp = nh // gh
    ph = PACK // hd
    npair = nh // ph

    @pl.when(pl.program_id(1) == 0)
    def _():
        xbuf[...] = jnp.zeros_like(xbuf)
        s_scr[...] = jnp.zeros_like(s_scr)

    sub = lax.broadcasted_iota(jnp.int32, (CARRY_ROWS, hd), 0)
    for part, (src, dst) in enumerate(((q_ref, qn_scr), (k_ref, kn_scr), (v_ref, vv_scr))):
        for h in range(nh):
            cs = slice(part * w + h * hd, part * w + (h + 1) * hd)
            x = src[:, h * hd:(h + 1) * hd].astype(F32)
            prev = xbuf[:, cs]
            y = cw_ref[CONV_TAPS - 1:CONV_TAPS, cs] * x
            for sft in range(1, CONV_TAPS):
                r = pltpu.roll(x, sft, 0)
                head_rows = jnp.where(sub < sft, pltpu.roll(prev, sft, 0), r[:CARRY_ROWS])
                r = jnp.concatenate([head_rows, r[CARRY_ROWS:]], axis=0)
                y = y + cw_ref[CONV_TAPS - 1 - sft:CONV_TAPS - sft, cs] * r
            xbuf[:, cs] = x[ts - CARRY_ROWS:]
            y = y * _sigmoid(y)
            if part < 2:
                y = y * (lax.rsqrt(jnp.sum(y * y, axis=-1, keepdims=True) + NORM_EPS)
                         * (hd ** -0.5 if part == 0 else 1.0))
            dst[:, h * hd:(h + 1) * hd] = y

    sm = sm_ref[...]
    g = -jnp.exp(alog_ref[...]) * _softplus(sm + dtb_ref[...])
    gc = _dot_exact_lhs(_chunk_cumsum_matrix(ts), g)
    beta = _sigmoid(sm)
    gcw_scr[...] = _dot_exact_rhs(gc, _bf01((LANES, w), lambda r, c: r == c // hd))
    btw_scr[...] = _dot_exact_rhs(beta, _bf01((LANES, w), lambda r, c: r == nh + c // hd))

    tile_p = (ts, PACK)
    incl_p = _mask01(tile_p, lambda r, c: r % CHUNK >= c % CHUNK)
    diag_p = _mask01(tile_p, lambda r, c: r % CHUNK == c % CHUNK)
    same_chunk = _bf01((ts, ts), lambda r, c: r // CHUNK == c // CHUNK)
    chunk_p = (CHUNK, PACK)
    strict_p = _mask01(chunk_p, lambda r, c: r > c % CHUNK)
    leaf_p = _mask01(chunk_p, lambda r, c: r // INV_LEAF == (c % CHUNK) // INV_LEAF)
    eye_p = jnp.where(_mask01(chunk_p, lambda r, c: r == c % CHUNK), 1.0, 0.0).astype(F32)
    head_p = [_mask01(chunk_p, lambda r, c, hh=hh: c // CHUNK == hh) for hh in range(gh)]
    bd01 = _bf01((PACK, PACK), lambda r, c: r // CHUNK == c // CHUNK)
    kbd01 = _bf01((PACK, gh * hd), lambda r, c: r // CHUNK == c // hd)

    dec_g, bcol_g = [], []
    for gi in range(ngrp):
        gcol = _dot_exact_rhs(gc, _bf01((LANES, PACK), lambda r, c, gi=gi: r == gi * gh + c // CHUNK))
        grow = _dot_exact_lhs(same_chunk, jnp.where(diag_p, gcol, 0.0))
        dec_g.append(jnp.exp(jnp.where(incl_p, gcol - grow, -jnp.inf)))
        bcol_g.append(_dot_exact_rhs(beta, _bf01((LANES, PACK), lambda r, c, gi=gi: r == nh + gi * gh + c // CHUNK)))

    chains = [(c, gi) for c in range(nch) for gi in range(ngrp)]
    lows, a_ps = [], []
    for c, gi in chains:
        rows = slice(c * CHUNK, (c + 1) * CHUNK)
        gcols = slice(gi * gh * hd, (gi + 1) * gh * hd)
        qg = qn_scr[rows, gcols].astype(BF16)
        kg = kn_scr[rows, gcols].astype(BF16)
        kbd = jnp.concatenate([kg] * gh, axis=0) * kbd01
        qk_kk = lax.dot_general(jnp.concatenate([qg, kg], axis=0), kbd, (((1,), (1,)), ((), ())),
                                preferred_element_type=F32)
        dec = dec_g[gi][rows]
        lows.append(jnp.where(strict_p, bcol_g[gi][rows] * qk_kk[CHUNK:] * dec, 0.0))
        a_ps.append(qk_kk[:CHUNK] * dec)

    tinvs = _packed_unit_lower_inverse(lows, eye_p, leaf_p, bd01)

    def blockrows(mat_p):
        return jnp.concatenate([jnp.where(m, mat_p, 0.0) for m in head_p], axis=0).astype(BF16)

    uws = []
    for (c, gi), tinv in zip(chains, tinvs):
        rows = slice(c * CHUNK, (c + 1) * CHUNK)
        rhs = []
        for hh in range(gh):
            hc = slice((gi * gh + hh) * hd, (gi * gh + hh + 1) * hd)
            bt = btw_scr[rows, hc]
            rhs.append(jnp.concatenate([bt * vv_scr[rows, hc],
                                        bt * jnp.exp(gcw_scr[rows, hc]) * kn_scr[rows, hc]], axis=1))
        uws.append(_dot(blockrows(tinv), jnp.concatenate(rhs, axis=0)))
    auws = [_dot(blockrows(a_p), uw) for a_p, uw in zip(a_ps, uws)]

    for (c, gi), uw, auw in zip(chains, uws, auws):
        rows = slice(c * CHUNK, (c + 1) * CHUNK)
        last = slice((c + 1) * CHUNK - 1, (c + 1) * CHUNK)
        for hh in range(gh):
            h = gi * gh + hh
            hc = slice(h * hd, (h + 1) * hd)
            hr = slice(hh * CHUNK, (hh + 1) * CHUNK)
            gcb = gcw_scr[rows, hc]
            au_scr[rows, hc] = auw[hr, :hd]
            qp_scr[rows, hc] = (qn_scr[rows, hc] * jnp.exp(gcb) - auw[hr, hd:]).astype(qp_scr.dtype)
            kd_t = (kn_scr[rows, hc] * jnp.exp(gcw_scr[last, hc] - gcb)).T
            bw = _dot(kd_t, uw[hr])
            sc = slice((h % ph) * hd, (h % ph + 1) * hd)
            bm_scr[c, h // ph, :, sc] = bw[:, :hd]
            w2_scr[c, h // ph, :, sc] = bw[:, hd:].astype(w2_scr.dtype)
    for c in range(nch):
        last = slice((c + 1) * CHUNK - 1, (c + 1) * CHUNK)
        ee_scr[c * CARRY_ROWS:(c + 1) * CARRY_ROWS, :] = jnp.broadcast_to(jnp.exp(gcw_scr[last, :]), (CARRY_ROWS, w))

    left = _mask01((hd, PACK), lambda r, c: c < hd)
    ng = ng_ref[...]

    states = [s_scr[p] for p in range(npair)]
    for c in range(nch):
        rows = slice(c * CHUNK, (c + 1) * CHUNK)
        ee = ee_scr[c * CARRY_ROWS:c * CARRY_ROWS + 1, :]
        prods = []
        for p in range(npair):
            pc = slice(p * PACK, (p + 1) * PACK)
            s = states[p]
            s_bd = jnp.concatenate([jnp.where(left, s, 0.0), jnp.where(left, 0.0, s)], axis=0).astype(BF16)
            lhs = jnp.concatenate([w2_scr[c, p], qp_scr[rows, pc]], axis=0)
            prods.append(jnp.dot(lhs, s_bd, preferred_element_type=F32))
        for p, r in enumerate(prods):
            pc = slice(p * PACK, (p + 1) * PACK)
            states[p] = ee[:, pc] * states[p] + bm_scr[c, p] - r[:hd]
            o = au_scr[rows, pc] + r[hd:]
            for side in range(ph):
                hc = slice((p * ph + side) * hd, (p * ph + side + 1) * hd)
                z = z_ref[rows, hc].astype(F32)
                o_ref[rows, hc] = (_rms(o[:, side * hd:(side + 1) * hd], ng) * (z * _sigmoid(z))).astype(o_ref.dtype)
    for p in range(npair):
        s_scr[p] = states[p]


def _gdn(proj, small, conv_w, alog, dtb, norm_g, *, bsz, seq, ts, nh, hd):
    t = bsz * seq
    nt = seq // ts
    w = nh * hd
    nch = ts // CHUNK
    assert PACK % CHUNK == 0 and PACK % hd == 0 and nh % (PACK // CHUNK) == 0 and nh % (PACK // hd) == 0
    tok = lambda cb: pl.BlockSpec((ts, w), lambda b, j: (b * nt + j, cb))
    full = lambda shape: pl.BlockSpec(shape, lambda b, j: (0, 0))
    return pl.pallas_call(
        functools.partial(_gdn_kernel, ts=ts, nh=nh, hd=hd),
        grid=(bsz, nt),
        in_specs=[tok(0), tok(1), tok(2), tok(3),
                  pl.BlockSpec((ts, LANES), lambda b, j: (b * nt + j, 0)),
                  full((CONV_TAPS, 3 * w)), full((1, LANES)), full((1, LANES)), full((1, hd))],
        out_specs=pl.BlockSpec((ts, w), lambda b, j: (b * nt + j, 0)),
        out_shape=jax.ShapeDtypeStruct((t, w), BF16),
        scratch_shapes=[
            pltpu.VMEM((CARRY_ROWS, 3 * w), F32),
            pltpu.VMEM((ts, w), F32), pltpu.VMEM((ts, w), F32), pltpu.VMEM((ts, w), F32),
            pltpu.VMEM((ts, w), F32), pltpu.VMEM((ts, w), F32),
            pltpu.VMEM((nch, nh * hd // PACK, hd, PACK), BF16),
            pltpu.VMEM((nch, nh * hd // PACK, hd, PACK), F32),
            pltpu.VMEM((ts, w), BF16), pltpu.VMEM((ts, w), F32),
            pltpu.VMEM((nch * CARRY_ROWS, w), F32),
            pltpu.VMEM((nh * hd // PACK, hd, PACK), F32),
        ],
        compiler_params=_params(("parallel", "arbitrary")),
        name="gdn",
    )(proj, proj, proj, proj, small, conv_w, alog, dtb, norm_g)


def _gla_kernel(q_ref, k_ref, v_ref, r_ref, sm_ref, wg_ref, bg_ref, ng_ref, o_ref, st_scr, *, ts, nh, dk, dv):
    nch = ts // CHUNK
    gh = PACK // CHUNK
    ngrp = nh // gh

    @pl.when(pl.program_id(1) == 0)
    def _():
        st_scr[...] = jnp.zeros_like(st_scr)

    lg = _dot(sm_ref[...], wg_ref[...]) + bg_ref[...]
    log_fg = (jnp.minimum(lg, 0.0) - jnp.log(1.0 + jnp.exp(-jnp.abs(lg)))) / GLA_GATE_TAU
    bc_all = _dot_exact_lhs(_chunk_cumsum_matrix(ts), log_fg)

    chunk_p = (CHUNK, PACK)
    incl_p = _mask01(chunk_p, lambda r, c: r >= c % CHUNK)
    head_p = [_mask01(chunk_p, lambda r, c, hh=hh: c // CHUNK == hh) for hh in range(gh)]
    kbd01 = _bf01((PACK, gh * dk), lambda r, c: r // CHUNK == c // dk)
    ng = ng_ref[...]
    scale = dk ** -0.5

    def blockrows(mat_p):
        return jnp.concatenate([jnp.where(m, mat_p, 0.0) for m in head_p], axis=0).astype(BF16)

    qd, kd, ge, a_ps = [], [], [], []
    for c in range(nch):
        rows = slice(c * CHUNK, (c + 1) * CHUNK)
        bc = bc_all[rows]
        bref = bc[CHUNK // 2:CHUNK // 2 + 1, :]
        bend = bc[CHUNK - 1:CHUNK, :]
        q = q_ref[rows, :].astype(F32) * scale
        k = k_ref[rows, :].astype(F32)
        qe = (q * jnp.exp(bc - bref)).astype(BF16)
        ke = (k * jnp.exp(bref - bc)).astype(BF16)
        qd.append((q * jnp.exp(bc)).astype(BF16))
        kd.append((k * jnp.exp(bend - bc)).astype(BF16))
        ge.append(jnp.exp(bend))
        for gi in range(ngrp):
            gcols = slice(gi * gh * dk, (gi + 1) * gh * dk)
            kebd = jnp.concatenate([ke[:, gcols]] * gh, axis=0) * kbd01
            a = lax.dot_general(qe[:, gcols], kebd, (((1,), (1,)), ((), ())), preferred_element_type=F32)
            a_ps.append(jnp.where(incl_p, a, 0.0))

    oi = []
    for c in range(nch):
        rows = slice(c * CHUNK, (c + 1) * CHUNK)
        for gi in range(ngrp):
            v4 = jnp.concatenate([v_ref[rows, (gi * gh + hh) * dv:(gi * gh + hh + 1) * dv] for hh in range(gh)], axis=0)
            oi.append(jnp.dot(blockrows(a_ps[c * ngrp + gi]), v4, preferred_element_type=F32))
    kvt = []
    for c in range(nch):
        rows = slice(c * CHUNK, (c + 1) * CHUNK)
        for h in range(nh):
            v_t = v_ref[rows, h * dv:(h + 1) * dv].astype(F32).T
            kvt.append(_dot(v_t, kd[c][:, h * dk:(h + 1) * dk]))

    st = [st_scr[h] for h in range(nh)]
    for c in range(nch):
        rows = slice(c * CHUNK, (c + 1) * CHUNK)
        for h in range(nh):
            ks = slice(h * dk, (h + 1) * dk)
            vs = slice(h * dv, (h + 1) * dv)
            hr = slice((h % gh) * CHUNK, (h % gh + 1) * CHUNK)
            o = _dot_nt(qd[c][:, ks], st[h]) + oi[c * ngrp + h // gh][hr]
            st[h] = ge[c][:, ks] * st[h] + kvt[c * nh + h]
            r = r_ref[rows, vs].astype(F32)
            o_ref[rows, vs] = (_rms(o, ng) * (r * _sigmoid(r))).astype(o_ref.dtype)
    for h in range(nh):
        st_scr[h] = st[h]


def _gla(proj, small, w_gate, b_gate, norm_g, *, bsz, seq, ts, nh, dk, dv, col0):
    t = bsz * seq
    nt = seq // ts
    kw, vw = nh * dk, nh * dv
    assert nh % (PACK // CHUNK) == 0
    spec = lambda width, start: pl.BlockSpec((ts, width), lambda b, j: (b * nt + j, start // width))
    full = lambda shape: pl.BlockSpec(shape, lambda b, j: (0, 0))
    return pl.pallas_call(
        functools.partial(_gla_kernel, ts=ts, nh=nh, dk=dk, dv=dv),
        grid=(bsz, nt),
        in_specs=[spec(kw, col0), spec(kw, col0 + kw), spec(vw, col0 + 2 * kw), spec(vw, col0 + 2 * kw + vw),
                  pl.BlockSpec((ts, LANES), lambda b, j: (b * nt + j, 0)),
                  full((LANES, kw)), full((1, kw)), full((1, dv))],
        out_specs=pl.BlockSpec((ts, vw), lambda b, j: (b * nt + j, 0)),
        out_shape=jax.ShapeDtypeStruct((t, vw), BF16),
        scratch_shapes=[pltpu.VMEM((nh, dv, dk), F32)],
        compiler_params=_params(("parallel", "arbitrary")),
        name="gla",
    )(proj, proj, proj, proj, small, w_gate, b_gate, norm_g)


def _memkv_kernel(m_ref, g_ref, wk_ref, wv_ref, k_ref, v_ref):
    m = _rms(m_ref[...], g_ref[...]).astype(BF16)
    k_ref[...] = jnp.dot(m, wk_ref[...], preferred_element_type=F32).astype(BF16)
    v_ref[...] = jnp.dot(m, wv_ref[...], preferred_element_type=F32).astype(BF16)


def _memkv(mem2, g, wk, wv, *, n_mem):
    t, d = mem2.shape
    tok = pl.BlockSpec((n_mem, d), lambda b: (b, 0))
    wsp = pl.BlockSpec((d, d), lambda b: (0, 0))
    return pl.pallas_call(
        _memkv_kernel,
        grid=(t // n_mem,),
        in_specs=[tok, pl.BlockSpec((1, d), lambda b: (0, 0)), wsp, wsp],
        out_specs=[tok, tok],
        out_shape=[jax.ShapeDtypeStruct((t, d), BF16)] * 2,
        compiler_params=_params(("parallel",)),
        name="memkv",
    )(mem2, g, wk, wv)


def _mix_xattn_kernel(x_ref, oa_ref, ob_ref, ga_ref, gb_ref, wa_ref, wb_ref, wo_ref,
                      g_ref, k_ref, v_ref, wq_ref, wxo_ref, out_ref, *, nh):
    ya = jnp.dot(oa_ref[...], wa_ref[...], preferred_element_type=F32)
    yb = jnp.dot(ob_ref[...], wb_ref[...], preferred_element_type=F32)
    merged = _sigmoid(ga_ref[...].astype(F32)) * ya + _sigmoid(gb_ref[...].astype(F32)) * yb
    x = x_ref[...] + _dot(merged, wo_ref[...])
    d = x.shape[-1]
    hd = d // nh
    q = _dot(_rms(x, g_ref[...]), wq_ref[...])
    cols = [slice(h * hd, (h + 1) * hd) for h in range(nh)]
    qb = q.astype(BF16)
    scores = [_dot_nt(qb[:, cs], k_ref[:, cs]) * (hd ** -0.5) for cs in cols]
    probs = []
    for s in scores:
        e = jnp.exp(s - jnp.max(s, axis=-1, keepdims=True))
        probs.append((e / jnp.sum(e, axis=-1, keepdims=True)).astype(BF16))
    o = jnp.concatenate([jnp.dot(p, v_ref[:, cs], preferred_element_type=F32) for p, cs in zip(probs, cols)], axis=1)
    out_ref[...] = x + _dot(o, wxo_ref[...])


def _mix_xattn(x2, o_a, o_b, proj, w_a, w_b, w_o, g, k_mem, v_mem, wq, wxo, *, bsz, seq, tm, n_mem, gate_col):
    t, d = x2.shape
    nt = seq // tm
    tok = lambda cb: pl.BlockSpec((tm, d), lambda b, j: (b * nt + j, cb))
    mem = pl.BlockSpec((n_mem, d), lambda b, j: (b, 0))
    wsp = pl.BlockSpec((d, d), lambda b, j: (0, 0))
    return pl.pallas_call(
        functools.partial(_mix_xattn_kernel, nh=XATTN_HEADS),
        grid=(bsz, nt),
        in_specs=[tok(0), tok(0), tok(0), tok(gate_col // d), tok(gate_col // d + 1), wsp, wsp, wsp,
                  pl.BlockSpec((1, d), lambda b, j: (0, 0)), mem, mem, wsp, wsp],
        out_specs=tok(0),
        out_shape=jax.ShapeDtypeStruct((t, d), F32),
        compiler_params=_params(("parallel", "parallel"), VMEM_LIMIT_FUSED),
        name="mix_xattn",
    )(x2, o_a, o_b, proj, proj, w_a, w_b, w_o, g, k_mem, v_mem, wq, wxo)


def _mlp_kernel(x_ref, g_ref, w1_ref, w2_ref, gf_ref, out_ref, h_scr, acc_scr, *, final_norm):
    j = pl.program_id(1)

    @pl.when(j == 0)
    def _():
        h_scr[...] = _rms(x_ref[...], g_ref[...]).astype(BF16)
        acc_scr[...] = x_ref[...]

    a = jnp.dot(h_scr[...], w1_ref[...], preferred_element_type=F32)
    acc_scr[...] += _dot(jnp.square(jnp.maximum(a, 0.0)), w2_ref[...])

    @pl.when(j == pl.num_programs(1) - 1)
    def _():
        y = acc_scr[...]
        out_ref[...] = _rms(y, gf_ref[...]) if final_norm else y


def _mlp(x2, g, w1, w2, g_final, *, tm, tf, final_norm):
    t, d = x2.shape
    ff = w1.shape[1]
    tok = pl.BlockSpec((tm, d), lambda i, j: (i, 0))
    vec = pl.BlockSpec((1, d), lambda i, j: (0, 0))
    return pl.pallas_call(
        functools.partial(_mlp_kernel, final_norm=final_norm),
        grid=(t // tm, ff // tf),
        in_specs=[tok, vec, pl.BlockSpec((d, tf), lambda i, j: (0, j)),
                  pl.BlockSpec((tf, d), lambda i, j: (j, 0)), vec],
        out_specs=tok,
        out_shape=jax.ShapeDtypeStruct((t, d), F32),
        scratch_shapes=[pltpu.VMEM((tm, d), BF16), pltpu.VMEM((tm, d), F32)],
        compiler_params=_params(("parallel", "arbitrary")),
        name="mlp",
    )(x2, g, w1, w2, g_final)


def _pad_lanes(v, offset):
    return jnp.zeros((1, LANES), F32).at[0, offset:offset + v.shape[0]].set(v.astype(F32))


def _tile(n, pref):
    while n % pref:
        pref //= 2
    return pref


def kernel(x, mem, norm_mix_g, w_in, gdn_conv_w, gdn_a_log, gdn_dt_bias, gdn_norm_g, gla_w_gate2, gla_b_gate, gla_norm_g, w_branch_gdn, w_branch_gla, w_out, norm_xattn_g, norm_mem_g, xattn_wq, xattn_wk, xattn_wv, xattn_wo, norm_mlp_g, mlp_w1, mlp_w2, norm_final_g):
    bsz, seq, d = x.shape
    n_mem = mem.shape[1]
    depth = w_in.shape[0]
    gdn_heads = gdn_a_log.shape[1]
    gdn_hd = gdn_norm_g.shape[1]
    gdn_w = gdn_heads * gdn_hd
    gla_kw = gla_b_gate.shape[1]
    gla_dv = gla_norm_g.shape[1]
    gla_vw = w_branch_gla.shape[1]
    gla_heads = gla_vw // gla_dv
    gla_dk = gla_kw // gla_heads
    rank = gla_w_gate2.shape[1]
    assert depth >= 1 and 2 * gdn_heads + rank <= LANES and seq % CHUNK == 0

    sizes = (gdn_w, gdn_w, gdn_w, gdn_w, gdn_heads, gdn_heads, gla_kw, gla_kw, gla_vw, gla_vw, rank, d, d)
    offs = [0]
    for s in sizes:
        offs.append(offs[-1] + s)
    assert offs[-1] == w_in.shape[2]
    big_cols = [0, 1, 2, 3, 6, 7, 8, 9, 11, 12]
    gla_col0 = 4 * gdn_w
    gate_col = gla_col0 + 2 * gla_kw + 2 * gla_vw

    t = bsz * seq
    x2 = x.reshape(t, d).astype(F32)
    mem2 = mem.reshape(bsz * n_mem, d).astype(F32)
    row = lambda v: v.reshape(1, -1).astype(F32)
    tm = _tile(t, 1024)
    ts = _tile(seq, 512)

    for i in range(depth):
        wi = w_in[i]
        w_big = jnp.concatenate([wi[:, offs[c]:offs[c + 1]] for c in big_cols], axis=1).astype(BF16)
        w_small = jnp.zeros((d, LANES), F32)
        w_small = w_small.at[:, 0:gdn_heads].set(wi[:, offs[4]:offs[5]])
        w_small = w_small.at[:, gdn_heads:2 * gdn_heads].set(wi[:, offs[5]:offs[6]])
        w_small = w_small.at[:, 2 * gdn_heads:2 * gdn_heads + rank].set(wi[:, offs[10]:offs[11]])
        w_gate = jnp.zeros((LANES, gla_kw), F32).at[2 * gdn_heads:2 * gdn_heads + rank].set(gla_w_gate2[i])

        proj, small = _inproj(x2, row(norm_mix_g[i]), w_big, w_small.astype(BF16), tm=tm, tn=_tile(w_big.shape[1], 2304))
        o_a = _gdn(proj, small, gdn_conv_w[i].astype(F32), _pad_lanes(gdn_a_log[i], 0),
                   _pad_lanes(gdn_dt_bias[i], 0), row(gdn_norm_g[i]),
                   bsz=bsz, seq=seq, ts=ts, nh=gdn_heads, hd=gdn_hd)
        o_b = _gla(proj, small, w_gate.astype(BF16), row(gla_b_gate[i]), row(gla_norm_g[i]),
                   bsz=bsz, seq=seq, ts=ts, nh=gla_heads, dk=gla_dk, dv=gla_dv, col0=gla_col0)
        k_mem, v_mem = _memkv(mem2, row(norm_mem_g[i]), xattn_wk[i].astype(BF16), xattn_wv[i].astype(BF16), n_mem=n_mem)
        x2 = _mix_xattn(x2, o_a, o_b, proj, w_branch_gdn[i].astype(BF16), w_branch_gla[i].astype(BF16),
                        w_out[i].astype(BF16), row(norm_xattn_g[i]), k_mem, v_mem, xattn_wq[i].astype(BF16),
                        xattn_wo[i].astype(BF16), bsz=bsz, seq=seq, tm=_tile(seq, 1024), n_mem=n_mem, gate_col=gate_col)

        last = i == depth - 1
        x2 = _mlp(x2, row(norm_mlp_g[i]), mlp_w1[i].astype(BF16), mlp_w2[i].astype(BF16), row(norm_final_g),
                  tm=tm, tf=_tile(mlp_w1.shape[2], 2048), final_norm=last)
    return x2.reshape(bsz, seq, d).astype(x.dtype)
```
